```python
import math
import jax, jax.numpy as jnp
from jax import lax
import numpy as np

D_MODEL = 2048
BATCH = 8
SEQ = 4096
DEPTH = 1
DEC_BATCH = 32
DEC_SEQ = 16
PAST_LEN = 1024

CHUNK = 64
Q_BLOCK = 128
D_A = D_MODEL // 2
D_B = D_MODEL - D_A
HEAD_V_A = 128
N_HEADS_A = D_A // HEAD_V_A
HEAD_QK_A = HEAD_V_A // 2
HEAD_B = 128
N_HEADS_B = D_B // HEAD_B
D_IN = 4 * D_A + 4 * D_B + N_HEADS_B
EPS = 1e-6

kernel_name = "hybrid_diffattn_fox_streaming_step"


def rmsnorm(x, g):
    xf = x.astype(jnp.float32)
    y = xf * lax.rsqrt(jnp.mean(xf * xf, axis=-1, keepdims=True) + EPS) * g.astype(jnp.float32)
    return y.astype(x.dtype)


def alibi_slopes():
    return 2.0 ** (-8.0 * jnp.arange(1, N_HEADS_A + 1, dtype=jnp.float32) / N_HEADS_A)


def project(x, g_norm, w_in, b_f):
    xn = rmsnorm(x, g_norm)
    z = jnp.einsum('btd,de->bte', xn, w_in)
    B, T, _ = z.shape
    qa = z[..., 0:D_A].reshape(B, T, 2 * N_HEADS_A, HEAD_QK_A)
    ka = z[..., D_A:2 * D_A].reshape(B, T, 2 * N_HEADS_A, HEAD_QK_A)
    va = z[..., 2 * D_A:3 * D_A].reshape(B, T, N_HEADS_A, HEAD_V_A)
    ga = z[..., 3 * D_A:4 * D_A]
    o = 4 * D_A
    qb = z[..., o:o + D_B].reshape(B, T, N_HEADS_B, HEAD_B)
    kb = z[..., o + D_B:o + 2 * D_B].reshape(B, T, N_HEADS_B, HEAD_B)
    vb = z[..., o + 2 * D_B:o + 3 * D_B].reshape(B, T, N_HEADS_B, HEAD_B)
    gb = z[..., o + 3 * D_B:o + 4 * D_B]
    fl = z[..., o + 4 * D_B:]
    logf = jax.nn.log_sigmoid(fl.astype(jnp.float32) + b_f.astype(jnp.float32))
    return qa, ka, va, ga, qb, kb, vb, gb, logf


def diff_attn(qpos, q, kpos, k, v, slopes, lam):
    s = jnp.einsum('bqhd,bkhd->bhqk', q, k, preferred_element_type=jnp.float32) * (HEAD_QK_A ** -0.5)
    B, _, Tq, Tk = s.shape
    s = s.reshape(B, N_HEADS_A, 2, Tq, Tk)
    dist = jnp.abs(qpos[:, None] - kpos[None, :]).astype(jnp.float32)
    allowed = (kpos[None, :] // CHUNK) <= (qpos[:, None] // CHUNK)
    bias = jnp.where(allowed[None], -slopes[:, None, None] * dist[None], -jnp.inf)
    p = jax.nn.softmax(s + bias[None, :, None], axis=-1)
    w = p[:, :, 0] - lam * p[:, :, 1]
    return jnp.einsum('bhqk,bkhd->bqhd', w.astype(v.dtype), v)


def fox_attn(qpos, q, cq, kpos, k, v, ck):
    s = jnp.einsum('bqhd,bkhd->bhqk', q, k, preferred_element_type=jnp.float32) * (HEAD_B ** -0.5)
    decay = jnp.transpose(cq, (0, 2, 1))[..., :, None] - jnp.transpose(ck, (0, 2, 1))[..., None, :]
    allowed = kpos[None, :] <= qpos[:, None]
    s = jnp.where(allowed[None, None], s + decay, -jnp.inf)
    p = jax.nn.softmax(s, axis=-1)
    return jnp.einsum('bhqk,bkhd->bqhd', p.astype(v.dtype), v)


def sweep(fn, qpos, *q_arrays):
    T = qpos.shape[0]
    if T <= Q_BLOCK:
        return fn(qpos, *q_arrays)
    nb = T // Q_BLOCK
    qp = qpos.reshape(nb, Q_BLOCK)
    qs = [jnp.moveaxis(a.reshape(a.shape[0], nb, Q_BLOCK, *a.shape[2:]), 1, 0) for a in q_arrays]
    out = lax.map(lambda args: fn(*args), (qp, *qs))
    out = jnp.moveaxis(out, 0, 1)
    return out.reshape(out.shape[0], T, *out.shape[3:])


def attend(qpos, kpos, qa, ka, va, qb, kb, vb, cq, ck, slopes, lam):
    oa = sweep(lambda qp, q: diff_attn(qp, q, kpos, ka, va, slopes, lam), qpos, qa)
    ob = sweep(lambda qp, q, c: fox_attn(qp, q, c, kpos, kb, vb, ck), qpos, qb, cq)
    return oa, ob


def merge(x, oa, ga, ob, gb, sub_gain, lam_init, w_out):
    B, T = x.shape[0], x.shape[1]
    oa = rmsnorm(oa, sub_gain) * (1.0 - lam_init)
    m = jnp.concatenate([jax.nn.silu(ga) * oa.reshape(B, T, D_A),
                         jax.nn.silu(gb) * ob.reshape(B, T, D_B)], axis=-1)
    return x + jnp.einsum('bte,ed->btd', m, w_out)


def setup_inputs(seed: int = 0) -> dict:
    key = jax.random.key(seed)
    ks = jax.random.split(key, 20)
    f32 = jnp.float32
    x_prompt = jax.random.normal(ks[0], (BATCH, SEQ, D_MODEL), f32)
    x_sample = jax.random.normal(ks[1], (DEC_BATCH, DEC_SEQ, D_MODEL), f32)
    cache_a_k = jax.random.normal(ks[2], (DEPTH, DEC_BATCH, PAST_LEN, 2 * N_HEADS_A, HEAD_QK_A), f32)
    cache_a_v = jax.random.normal(ks[3], (DEPTH, DEC_BATCH, PAST_LEN, N_HEADS_A, HEAD_V_A), f32)
    cache_b_k = jax.random.normal(ks[4], (DEPTH, DEC_BATCH, PAST_LEN, N_HEADS_B, HEAD_B), f32)
    cache_b_v = jax.random.normal(ks[5], (DEPTH, DEC_BATCH, PAST_LEN, N_HEADS_B, HEAD_B), f32)
    cache_b_logf = jax.nn.log_sigmoid(jax.random.normal(ks[6], (DEPTH, DEC_BATCH, PAST_LEN, N_HEADS_B), f32))
    g_norm = 1.0 + 0.02 * jax.random.normal(ks[7], (DEPTH, D_MODEL), f32)
    w_in = jax.random.normal(ks[8], (DEPTH, D_MODEL, D_IN), f32) * D_MODEL ** -0.5
    b_f = 0.1 * jax.random.normal(ks[9], (DEPTH, N_HEADS_B), f32)
    lam_q1 = 0.1 * jax.random.normal(ks[10], (DEPTH, HEAD_QK_A), f32)
    lam_k1 = 0.1 * jax.random.normal(ks[11], (DEPTH, HEAD_QK_A), f32)
    lam_q2 = 0.1 * jax.random.normal(ks[12], (DEPTH, HEAD_QK_A), f32)
    lam_k2 = 0.1 * jax.random.normal(ks[13], (DEPTH, HEAD_QK_A), f32)
    sub_gain = 1.0 + 0.02 * jax.random.normal(ks[14], (DEPTH, HEAD_V_A), f32)
    w_out = jax.random.normal(ks[15], (DEPTH, D_MODEL, D_MODEL), f32) * D_MODEL ** -0.5
    g_final = 1.0 + 0.02 * jax.random.normal(ks[16], (D_MODEL,), f32)
    return {"x_prompt": x_prompt, "x_sample": x_sample,
            "cache_a_k": cache_a_k, "cache_a_v": cache_a_v,
            "cache_b_k": cache_b_k, "cache_b_v": cache_b_v, "cache_b_logf": cache_b_logf,
            "g_norm": g_norm, "w_in": w_in, "b_f": b_f,
            "lam_q1": lam_q1, "lam_k1": lam_k1, "lam_q2": lam_q2, "lam_k2": lam_k2,
            "sub_gain": sub_gain, "w_out": w_out, "g_final": g_final}


def reference(x_prompt, x_sample, cache_a_k, cache_a_v, cache_b_k, cache_b_v, cache_b_logf,
              g_norm, w_in, b_f, lam_q1, lam_k1, lam_q2, lam_k2, sub_gain, w_out, g_final):
    f32 = jnp.float32
    slopes = alibi_slopes()
    S_p = x_prompt.shape[1]
    S_s = x_sample.shape[1]
    L = cache_a_k.shape[2]
    pos_p = jnp.arange(S_p)
    kpos_s = jnp.arange(L + S_s)
    qpos_s = L + jnp.arange(S_s)
    h_p, h_s = x_prompt, x_sample
    pak, pav, pbk, pbv, pbf = [], [], [], [], []
    sak, sav, sbk, sbv, sbf = [], [], [], [], []
    for l in range(DEPTH):
        lam_init = 0.8 - 0.6 * math.exp(-0.3 * l)
        lam = (jnp.exp(jnp.sum(lam_q1[l].astype(f32) * lam_k1[l].astype(f32)))
               - jnp.exp(jnp.sum(lam_q2[l].astype(f32) * lam_k2[l].astype(f32))) + lam_init)
        qa, ka, va, ga, qb, kb, vb, gb, logf = project(h_p, g_norm[l], w_in[l], b_f[l])
        c = jnp.cumsum(logf, axis=1)
        oa, ob = attend(pos_p, pos_p, qa, ka, va, qb, kb, vb, c, c, slopes, lam)
        h_p = merge(h_p, oa, ga, ob, gb, sub_gain[l], lam_init, w_out[l])
        pak.append(ka); pav.append(va); pbk.append(kb); pbv.append(vb); pbf.append(logf)
        qa2, ka2, va2, ga2, qb2, kb2, vb2, gb2, logf2 = project(h_s, g_norm[l], w_in[l], b_f[l])
        ka_all = jnp.concatenate([cache_a_k[l].astype(ka2.dtype), ka2], axis=1)
        va_all = jnp.concatenate([cache_a_v[l].astype(va2.dtype), va2], axis=1)
        kb_all = jnp.concatenate([cache_b_k[l].astype(kb2.dtype), kb2], axis=1)
        vb_all = jnp.concatenate([cache_b_v[l].astype(vb2.dtype), vb2], axis=1)
        c_all = jnp.cumsum(jnp.concatenate([cache_b_logf[l].astype(f32), logf2], axis=1), axis=1)
        oa2, ob2 = attend(qpos_s, kpos_s, qa2, ka_all, va_all, qb2, kb_all, vb_all,
                          c_all[:, L:], c_all, slopes, lam)
        h_s = merge(h_s, oa2, ga2, ob2, gb2, sub_gain[l], lam_init, w_out[l])
        sak.append(ka2); sav.append(va2); sbk.append(kb2); sbv.append(vb2); sbf.append(logf2)
    y_prompt = rmsnorm(h_p, g_final)
    y_sample = rmsnorm(h_s, g_final)
    return (y_prompt, y_sample,
            jnp.stack(pak), jnp.stack(pav), jnp.stack(pbk), jnp.stack(pbv), jnp.stack(pbf),
            jnp.stack(sak), jnp.stack(sav), jnp.stack(sbk), jnp.stack(sbv), jnp.stack(sbf))
```

```python
import functools
import math

import jax
import jax.numpy as jnp
from jax import lax
from jax.experimental import pallas as pl
from jax.experimental.pallas import tpu as pltpu

F32 = jnp.float32
BF16 = jnp.bfloat16

EPS = 1e-6
CHUNK = 64
HEAD = 128
NEG = -1e30
LANES = 128
VMEM_LIMIT = 56 * 1024 * 1024

_NT = (((1,), (1,)), ((), ()))


def _cparams(n_axes):
    return pltpu.CompilerParams(
        dimension_semantics=("arbitrary",) * n_axes, vmem_limit_bytes=VMEM_LIMIT)


def _rms_scale(x):
    return lax.rsqrt(jnp.mean(x * x, axis=-1, keepdims=True) + EPS)


def _proj_kernel(x_ref, g_ref, w_ref, o_ref, *, scale):
    x = x_ref[...]
    xn = (x * _rms_scale(x) * g_ref[...]).astype(BF16)
    z = jnp.dot(xn, w_ref[...], preferred_element_type=F32)
    if scale != 1.0:
        z = z * scale
    o_ref[...] = z.astype(o_ref.dtype)


def _proj_call(x2, g, w_bf, slab, width, scale, out_dtype, tm):
    m, d = x2.shape
    return pl.pallas_call(
        functools.partial(_proj_kernel, scale=scale),
        grid=(m // tm,),
        in_specs=[pl.BlockSpec((tm, d), lambda i: (i, 0)),
                  pl.BlockSpec((1, d), lambda i: (0, 0)),
                  pl.BlockSpec((d, width), lambda i: (0, slab))],
        out_specs=pl.BlockSpec((tm, width), lambda i: (i, 0)),
        out_shape=jax.ShapeDtypeStruct((m, width), out_dtype),
        compiler_params=_cparams(1),
        name=f"proj_slab{slab}",
    )(x2, g, w_bf)


def _logf_kernel(x_ref, g_ref, whi_ref, wlo_ref, b_ref, o_ref, ot_ref, *, nh):
    x = x_ref[...]
    xn = x * _rms_scale(x) * g_ref[...]
    hi = xn.astype(BF16)
    lo = (xn - hi.astype(F32)).astype(BF16)
    whi = whi_ref[...]
    fl = (jnp.dot(hi, whi, preferred_element_type=F32)
          + jnp.dot(hi, wlo_ref[...], preferred_element_type=F32)
          + jnp.dot(lo, whi, preferred_element_type=F32))
    t = fl + b_ref[...]
    logf = jnp.minimum(t, 0.0) - jnp.log(1.0 + jnp.exp(-jnp.abs(t)))
    o_ref[...] = logf[:, :nh]
    ot_ref[...] = logf.T[:nh, :]


def _logf_call(x2, g, wf_hi, wf_lo, b_pad, nh, tm):
    m, d = x2.shape
    return pl.pallas_call(
        functools.partial(_logf_kernel, nh=nh),
        grid=(m // tm,),
        in_specs=[pl.BlockSpec((tm, d), lambda i: (i, 0)),
                  pl.BlockSpec((1, d), lambda i: (0, 0)),
                  pl.BlockSpec((d, LANES), lambda i: (0, 0)),
                  pl.BlockSpec((d, LANES), lambda i: (0, 0)),
                  pl.BlockSpec((1, LANES), lambda i: (0, 0))],
        out_specs=[pl.BlockSpec((tm, nh), lambda i: (i, 0)),
                   pl.BlockSpec((nh, tm), lambda i: (0, i))],
        out_shape=[jax.ShapeDtypeStruct((m, nh), F32),
                   jax.ShapeDtypeStruct((nh, m), F32)],
        compiler_params=_cparams(1),
        name="logf",
    )(x2, g, wf_hi, wf_lo, b_pad)


def _cumsum_kernel(x_ref, u_ref, o_ref, carry_ref, *, nper):
    @pl.when(pl.program_id(0) % nper == 0)
    def _():
        carry_ref[...] = jnp.zeros_like(carry_ref)

    x = x_ref[...]
    x1 = x.astype(BF16)
    r1 = x - x1.astype(F32)
    x2 = r1.astype(BF16)
    x3 = (r1 - x2.astype(F32)).astype(BF16)
    u = u_ref[...]
    y = (jnp.dot(x1, u, preferred_element_type=F32)
         + jnp.dot(x2, u, preferred_element_type=F32)
         + jnp.dot(x3, u, preferred_element_type=F32)) + carry_ref[...]
    o_ref[...] = y
    ch = y.shape[1]
    carry_ref[...] = jnp.broadcast_to(y[:, ch - 1:ch], y.shape)


def _cumsum_call(xt, ch, nper):
    nh, n = xt.shape
    tri = (jnp.arange(ch)[:, None] <= jnp.arange(ch)[None, :]).astype(BF16)
    return pl.pallas_call(
        functools.partial(_cumsum_kernel, nper=nper),
        grid=(n // ch,),
        in_specs=[pl.BlockSpec((nh, ch), lambda i: (0, i)),
                  pl.BlockSpec((ch, ch), lambda i: (0, 0))],
        out_specs=pl.BlockSpec((nh, ch), lambda i: (0, i)),
        out_shape=jax.ShapeDtypeStruct((nh, n), F32),
        scratch_shapes=[pltpu.VMEM((nh, ch), F32)],
        compiler_params=_cparams(1),
        name="cumsum",
    )(xt, tri)


def _widen(col, width):
    if width <= LANES:
        return col[:, :width]
    return jnp.concatenate([col] * (width // LANES), axis=1)


def _attn_kernel(*refs, mode, tq, tk, n_past, self_kv, one_minus_lam_init):
    it = iter(refs)
    q_ref, kp_ref, vp_ref = next(it), next(it), next(it)
    if not self_kv:
        kd_ref, vd_ref = next(it), next(it)
    g_ref, tile_ref = next(it), next(it)
    if mode == "diff":
        slope_ref, lamv_ref, subg_ref = next(it), next(it), next(it)
    else:
        cp_ref = next(it)
        if not self_kv:
            cd_ref = next(it)
    o_ref = next(it)
    kb_ref, vb_ref, m_ref, l_ref, acc_ref = next(it), next(it), next(it), next(it), next(it)

    h = pl.program_id(1)
    i = pl.program_id(2)
    npast = i if n_past is None else n_past

    @pl.when(i == 0)
    def _():
        kb_ref[...] = kp_ref[0].astype(BF16)
        vb_ref[...] = vp_ref[0].astype(BF16)

    q = q_ref[0]
    if mode == "diff":
        lane = lax.broadcasted_iota(jnp.int32, q.shape, 1)
        zero = jnp.zeros_like(q)
        qs = jnp.concatenate([jnp.where(lane < HEAD // 2, q, zero),
                              jnp.where(lane >= HEAD // 2, q, zero)], axis=0)
        slope = slope_ref[h]
    else:
        qs = q
    rows = qs.shape[0]

    m_ref[...] = jnp.full(m_ref.shape, NEG, F32)
    l_ref[...] = jnp.zeros(l_ref.shape, F32)
    acc_ref[...] = jnp.zeros(acc_ref.shape, F32)

    def flash_step(kblk, vblk, bias):
        s = lax.dot_general(qs, kblk, _NT, preferred_element_type=F32) + bias
        width = s.shape[1]
        m_prev = m_ref[...]
        m_new = jnp.maximum(m_prev, jnp.max(s, axis=1, keepdims=True))
        alpha = jnp.exp(m_prev - m_new)
        p = jnp.exp(s - _widen(m_new, width))
        l_ref[...] = alpha * l_ref[...] + jnp.sum(p, axis=1, keepdims=True)
        acc_ref[...] = alpha * acc_ref[...] + jnp.dot(
            p.astype(BF16), vblk, preferred_element_type=F32)
        m_ref[...] = m_new

    if self_kv:
        d0 = pl.multiple_of(i * tq, tq)
        kdiag = kb_ref[pl.ds(d0, tq), :]
        vdiag = vb_ref[pl.ds(d0, tq), :]
    else:
        kdiag = kd_ref[0].astype(BF16)
        vdiag = vd_ref[0].astype(BF16)
    tile = tile_ref[...]
    if mode == "diff":
        dtile = slope * tile
        dbias = jnp.concatenate([dtile, dtile], axis=0)
    else:
        cdiag = cp_ref[0, 0, pl.ds(i, 1), :] if self_kv else cd_ref[0, 0]
        cref = cdiag[:, 0:1]
        dbias = tile + (cref - cdiag)
    flash_step(kdiag, vdiag, dbias)

    col = lax.broadcasted_iota(jnp.int32, (1, tk), 1).astype(F32)

    def past_step(j, carry):
        k0 = pl.multiple_of(j * tk, tk)
        kblk = kb_ref[pl.ds(k0, tk), :]
        vblk = vb_ref[pl.ds(k0, tk), :]
        if mode == "diff":
            bias = slope * (col + ((j - npast) * tk).astype(F32))
        else:
            bias = cref - cp_ref[0, 0, pl.ds(j, 1), :]
        flash_step(kblk, vblk, bias)
        return carry

    lax.fori_loop(0, npast, past_step, 0)

    o = acc_ref[...] / l_ref[...]
    g = g_ref[0].astype(F32)
    gate = g / (1.0 + jnp.exp(-g))
    if mode == "diff":
        lv = lamv_ref[...]
        lam = (jnp.exp(jnp.sum(lv[0:1] * lv[1:2], axis=1, keepdims=True))
               - jnp.exp(jnp.sum(lv[2:3] * lv[3:4], axis=1, keepdims=True))
               + (1.0 - one_minus_lam_init))
        o = o[:tq] - lam * o[tq:]
        o = o * _rms_scale(o) * subg_ref[...] * one_minus_lam_init
    o_ref[0] = (gate * o).astype(o_ref.dtype)


def _attn_call(mode, q, kp, vp, kd, vd, g, tile, extra, *, tq, tk, self_kv, lam_init=0.0):
    b, tqt, w = q.shape
    lp = kp.shape[1]
    nh = w // HEAD
    nq = tqt // tq
    n_past = None if self_kv else lp // tk
    rows = 2 * tq if mode == "diff" else tq

    qmap = lambda bi, hi, ii: (bi, ii, hi)
    pmap = lambda bi, hi, ii: (bi, 0, hi)
    in_specs = [pl.BlockSpec((1, tq, HEAD), qmap),
                pl.BlockSpec((1, lp, HEAD), pmap),
                pl.BlockSpec((1, lp, HEAD), pmap)]
    args = [q, kp, vp]
    if not self_kv:
        in_specs += [pl.BlockSpec((1, tq, HEAD), qmap)] * 2
        args += [kd, vd]
    in_specs += [pl.BlockSpec((1, tq, HEAD), qmap),
                 pl.BlockSpec((tq, tq), lambda bi, hi, ii: (0, 0))]
    args += [g, tile]
    if mode == "diff":
        slopes, lamv, subg = extra
        in_specs += [pl.BlockSpec(memory_space=pltpu.SMEM),
                     pl.BlockSpec((4, LANES), lambda bi, hi, ii: (0, 0)),
                     pl.BlockSpec((1, HEAD), lambda bi, hi, ii: (0, 0))]
        args += [slopes, lamv, subg]
    else:
        cp, cd = extra
        in_specs += [pl.BlockSpec((1, 1, cp.shape[2], tk), lambda bi, hi, ii: (hi, bi, 0, 0))]
        args += [cp]
        if not self_kv:
            in_specs += [pl.BlockSpec((1, 1, 1, tq), lambda bi, hi, ii: (hi, bi, 0, 0))]
            args += [cd]

    kern = functools.partial(_attn_kernel, mode=mode, tq=tq, tk=tk, n_past=n_past,
                             self_kv=self_kv, one_minus_lam_init=1.0 - lam_init)
    return pl.pallas_call(
        kern,
        grid=(b, nh, nq),
        in_specs=in_specs,
        out_specs=pl.BlockSpec((1, tq, HEAD), qmap),
        out_shape=jax.ShapeDtypeStruct((b, tqt, w), BF16),
        scratch_shapes=[pltpu.VMEM((lp, HEAD), BF16), pltpu.VMEM((lp, HEAD), BF16),
                        pltpu.VMEM((rows, LANES), F32), pltpu.VMEM((rows, LANES), F32),
                        pltpu.VMEM((rows, HEAD), F32)],
        compiler_params=_cparams(3),
        name=f"attn_{mode}_{'self' if self_kv else 'cache'}",
    )(*args)


def _out_kernel(x_ref, ma_ref, mb_ref, wa_ref, wb_ref, gf_ref, o_ref):
    hres = (x_ref[...]
            + jnp.dot(ma_ref[...], wa_ref[...], preferred_element_type=F32)
            + jnp.dot(mb_ref[...], wb_ref[...], preferred_element_type=F32))
    o_ref[...] = hres * _rms_scale(hres) * gf_ref[...]


def _out_call(x2, ma, mb, wa, wb, gf, tm):
    m, d = x2.shape
    da, db = ma.shape[1], mb.shape[1]
    return pl.pallas_call(
        _out_kernel,
        grid=(m // tm,),
        in_specs=[pl.BlockSpec((tm, d), lambda i: (i, 0)),
                  pl.BlockSpec((tm, da), lambda i: (i, 0)),
                  pl.BlockSpec((tm, db), lambda i: (i, 0)),
                  pl.BlockSpec((da, d), lambda i: (0, 0)),
                  pl.BlockSpec((db, d), lambda i: (0, 0)),
                  pl.BlockSpec((1, d), lambda i: (0, 0))],
        out_specs=pl.BlockSpec((tm, d), lambda i: (i, 0)),
        out_shape=jax.ShapeDtypeStruct((m, d), F32),
        compiler_params=_cparams(1),
        name="out_proj",
    )(x2, ma, mb, wa, wb, gf)


def _diff_tile(tq):
    r = jnp.arange(tq)[:, None]
    c = jnp.arange(tq)[None, :]
    allowed = (c // CHUNK) <= (r // CHUNK)
    return jnp.where(allowed, (r - jnp.abs(r - c)).astype(F32), NEG)


def _causal_tile(tq):
    r = jnp.arange(tq)[:, None]
    c = jnp.arange(tq)[None, :]
    return jnp.where(c <= r, 0.0, NEG).astype(F32)


def _layer(x, past, params, *, tm, tq, tk, cum_ch):
    (g_norm, w_bf, wf_hi, wf_lo, b_pad, slopes, lamv, subg, lam_init,
     wo_a, wo_b, g_final, d_a, nh_b) = params
    b, s, d = x.shape
    x2 = x.reshape(b * s, d)
    qk_scale_a = float((HEAD // 2) ** -0.5)
    qk_scale_b = float(HEAD ** -0.5)

    proj = lambda slab, scale, dt: _proj_call(x2, g_norm, w_bf, slab, d_a, scale, dt, tm)
    qa = proj(0, qk_scale_a, BF16)
    ka = proj(1, 1.0, F32)
    va = proj(2, 1.0, F32)
    ga = proj(3, 1.0, BF16)
    qb = proj(4, qk_scale_b, BF16)
    kb = proj(5, 1.0, F32)
    vb = proj(6, 1.0, F32)
    gb = proj(7, 1.0, BF16)
    logf, logf_t = _logf_call(x2, g_norm, wf_hi, wf_lo, b_pad, nh_b, tm)

    r3 = lambda a: a.reshape(b, s, a.shape[-1])
    if past is None:
        c = _cumsum_call(logf_t, cum_ch, s // cum_ch)
        c4 = c.reshape(nh_b, b, s // tk, tk)
        ma = _attn_call("diff", r3(qa), r3(ka), r3(va), None, None, r3(ga), _diff_tile(tq),
                        (slopes, lamv, subg), tq=tq, tk=tk, self_kv=True, lam_init=lam_init)
        mb = _attn_call("fox", r3(qb), r3(kb), r3(vb), None, None, r3(gb), _causal_tile(tq),
                        (c4, None), tq=tq, tk=tk, self_kv=True)
    else:
        cak, cav, cbk, cbv, cbf = past
        lp = cak.shape[1]
        assert lp % CHUNK == 0 and lp % tk == 0 and s == tq <= CHUNK
        tot = lp + s
        pad = (-tot) % cum_ch
        lf_all = jnp.concatenate(
            [jnp.transpose(cbf, (2, 0, 1)), logf_t.reshape(nh_b, b, s),
             jnp.zeros((nh_b, b, pad), F32)], axis=2)
        c = _cumsum_call(lf_all.reshape(nh_b, b * (tot + pad)), cum_ch, (tot + pad) // cum_ch)
        c = c.reshape(nh_b, b, tot + pad)
        cp = c[:, :, :lp].reshape(nh_b, b, lp // tk, tk)
        cd = c[:, :, lp:tot].reshape(nh_b, b, 1, s)
        flat = lambda a: a.reshape(b, lp, -1)
        ma = _attn_call("diff", r3(qa), flat(cak), flat(cav), r3(ka), r3(va), r3(ga),
                        _diff_tile(tq), (slopes, lamv, subg), tq=tq, tk=tk, self_kv=False,
                        lam_init=lam_init)
        mb = _attn_call("fox", r3(qb), flat(cbk), flat(cbv), r3(kb), r3(vb), r3(gb),
                        _causal_tile(tq), (cp, cd), tq=tq, tk=tk, self_kv=False)

    y = _out_call(x2, ma.reshape(b * s, -1), mb.reshape(b * s, -1), wo_a, wo_b, g_final,
                  min(tm, 512))
    return y.reshape(b, s, d), ka, va, kb, vb, logf


def kernel(x_prompt, x_sample, cache_a_k, cache_a_v, cache_b_k, cache_b_v, cache_b_logf,
           g_norm, w_in, b_f, lam_q1, lam_k1, lam_q2, lam_k2, sub_gain, w_out, g_final):
    depth, d = g_norm.shape
    assert depth == 1
    nh_a2, hqk = cache_a_k.shape[3], cache_a_k.shape[4]
    nh_a = nh_a2 // 2
    nh_b = cache_b_k.shape[3]
    d_a = nh_a2 * hqk
    assert 2 * hqk == HEAD and cache_b_k.shape[4] == HEAD and d == 2 * d_a
    lyr = 0
    lam_init = 0.8 - 0.6 * math.exp(-0.3 * lyr)

    w = w_in[lyr]
    w_bf = w.astype(BF16)
    wf = jnp.pad(w[:, 8 * d_a:], ((0, 0), (0, LANES - nh_b)))
    wf_hi = wf.astype(BF16)
    wf_lo = (wf - wf_hi.astype(F32)).astype(BF16)
    b_pad = jnp.pad(b_f[lyr].astype(F32), (0, LANES - nh_b)).reshape(1, LANES)
    slopes = 2.0 ** (-8.0 * jnp.arange(1, nh_a + 1, dtype=F32) / nh_a)
    lamv = jnp.pad(jnp.stack([lam_q1[lyr], lam_k1[lyr], lam_q2[lyr], lam_k2[lyr]]).astype(F32),
                   ((0, 0), (0, LANES - hqk)))
    subg = sub_gain[lyr].astype(F32).reshape(1, HEAD)
    wo = w_out[lyr].astype(BF16)
    params = (g_norm[lyr].reshape(1, d), w_bf, wf_hi, wf_lo, b_pad, slopes, lamv, subg, lam_init,
              wo[:d_a], wo[d_a:], g_final.reshape(1, d), d_a, nh_b)

    bp, sp, _ = x_prompt.shape
    bs, ss, _ = x_sample.shape
    yp, pak, pav, pbk, pbv, pbf = _layer(x_prompt, None, params, tm=1024, tq=512, tk=512,
                                         cum_ch=512)
    past = (cache_a_k[lyr], cache_a_v[lyr], cache_b_k[lyr], cache_b_v[lyr], cache_b_logf[lyr])
    ys, sak, sav, sbk, sbv, sbf = _layer(x_sample, past, params, tm=bs * ss, tq=ss, tk=512,
                                         cum_ch=128)

    shp = lambda a, bb, s_, n, e: a.reshape(1, bb, s_, n, e)
    return (yp, ys,
            shp(pak, bp, sp, nh_a2, hqk), shp(pav, bp, sp, nh_a, HEAD),
            shp(pbk, bp, sp, nh_b, HEAD), shp(pbv, bp, sp, nh_b, HEAD),
            pbf.reshape(1, bp, sp, nh_b),
            shp(sak, bs, ss, nh_a2, hqk), shp(sav, bs, ss, nh_a, HEAD),
            shp(sbk, bs, ss, nh_b, HEAD), shp(sbv, bs, ss, nh_b, HEAD),
            sbf.reshape(1, bs, ss, nh_b))
```

```python
import functools
import math

import jax
import jax.numpy as jnp
from jax import lax
from jax.experimental import pallas as pl
from jax.experimental.pallas import tpu as pltpu

F32 = jnp.float32
BF16 = jnp.bfloat16

EPS = 1e-6
CHUNK = 64
HEAD = 128
NEG = -1e30
LANES = 128
LOG2E = math.log2(math.e)
N_BIAS = 3
QCOLS = 256
VMEM_LIMIT = 56 * 1024 * 1024

_NT = (((1,), (1,)), ((), ()))


def _cparams(n_axes):
    return pltpu.CompilerParams(
        dimension_semantics=("arbitrary",) * n_axes, vmem_limit_bytes=VMEM_LIMIT)


def _rms_scale(x):
    return lax.rsqrt(jnp.mean(x * x, axis=-1, keepdims=True) + EPS)


def _split3(x):
    x1 = x.astype(BF16)
    r1 = x - x1.astype(F32)
    x2 = r1.astype(BF16)
    x3 = (r1 - x2.astype(F32)).astype(BF16)
    return x1, x2, x3


def _proj_kernel(x_ref, g_ref, w_ref, o_ref, *, scale):
    x = x_ref[...]
    xn = (x * _rms_scale(x) * g_ref[...]).astype(BF16)
    z = jnp.dot(xn, w_ref[...], preferred_element_type=F32)
    if scale != 1.0:
        z = z * scale
    o_ref[...] = z.astype(o_ref.dtype)


def _proj_call(x2, g, w_bf, slab, width, scale, out_dtype, tm):
    m, d = x2.shape
    return pl.pallas_call(
        functools.partial(_proj_kernel, scale=scale),
        grid=(m // tm,),
        in_specs=[pl.BlockSpec((tm, d), lambda i: (i, 0)),
                  pl.BlockSpec((1, d), lambda i: (0, 0)),
                  pl.BlockSpec((d, width), lambda i: (0, slab))],
        out_specs=pl.BlockSpec((tm, width), lambda i: (i, 0)),
        out_shape=jax.ShapeDtypeStruct((m, width), out_dtype),
        compiler_params=_cparams(1),
        name=f"proj_slab{slab}",
    )(x2, g, w_bf)


def _logf_kernel(x_ref, g_ref, whi_ref, wlo_ref, b_ref, o_ref, ot_ref, *, nh):
    x = x_ref[...]
    xn = x * _rms_scale(x) * g_ref[...]
    hi = xn.astype(BF16)
    lo = (xn - hi.astype(F32)).astype(BF16)
    whi = whi_ref[...]
    fl = (jnp.dot(hi, whi, preferred_element_type=F32)
          + jnp.dot(hi, wlo_ref[...], preferred_element_type=F32)
          + jnp.dot(lo, whi, preferred_element_type=F32))
    t = fl + b_ref[...]
    logf = jnp.minimum(t, 0.0) - jnp.log(1.0 + jnp.exp(-jnp.abs(t)))
    o_ref[...] = logf[:, :nh]
    ot_ref[...] = logf.T[:nh, :]


def _logf_call(x2, g, wf_hi, wf_lo, b_pad, nh, tm):
    m, d = x2.shape
    return pl.pallas_call(
        functools.partial(_logf_kernel, nh=nh),
        grid=(m // tm,),
        in_specs=[pl.BlockSpec((tm, d), lambda i: (i, 0)),
                  pl.BlockSpec((1, d), lambda i: (0, 0)),
                  pl.BlockSpec((d, LANES), lambda i: (0, 0)),
                  pl.BlockSpec((d, LANES), lambda i: (0, 0)),
                  pl.BlockSpec((1, LANES), lambda i: (0, 0))],
        out_specs=[pl.BlockSpec((tm, nh), lambda i: (i, 0)),
                   pl.BlockSpec((nh, tm), lambda i: (0, i))],
        out_shape=[jax.ShapeDtypeStruct((m, nh), F32),
                   jax.ShapeDtypeStruct((nh, m), F32)],
        compiler_params=_cparams(1),
        name="logf",
    )(x2, g, wf_hi, wf_lo, b_pad)


def _cumsum_row_kernel(x_ref, u_ref, o_ref, carry_ref, *, nper):
    @pl.when(pl.program_id(0) % nper == 0)
    def _():
        carry_ref[...] = jnp.zeros_like(carry_ref)

    x1, x2, x3 = _split3(x_ref[...])
    u = u_ref[...]
    y = (jnp.dot(x1, u, preferred_element_type=F32)
         + jnp.dot(x2, u, preferred_element_type=F32)
         + jnp.dot(x3, u, preferred_element_type=F32)) + carry_ref[...]
    o_ref[...] = y
    ch = y.shape[1]
    carry_ref[...] = jnp.broadcast_to(y[:, ch - 1:ch], y.shape)


def _cumsum_row_call(xt, ch, nper):
    nh, n = xt.shape
    tri = (jnp.arange(ch)[:, None] <= jnp.arange(ch)[None, :]).astype(BF16)
    return pl.pallas_call(
        functools.partial(_cumsum_row_kernel, nper=nper),
        grid=(n // ch,),
        in_specs=[pl.BlockSpec((nh, ch), lambda i: (0, i)),
                  pl.BlockSpec((ch, ch), lambda i: (0, 0))],
        out_specs=pl.BlockSpec((nh, ch), lambda i: (0, i)),
        out_shape=jax.ShapeDtypeStruct((nh, n), F32),
        scratch_shapes=[pltpu.VMEM((nh, ch), F32)],
        compiler_params=_cparams(1),
        name="cumsum_row",
    )(xt, tri)


def _cumsum_col_kernel(x_ref, l_ref, o_ref, *, ch):
    ltri = l_ref[...]
    carry = jnp.zeros((1, x_ref.shape[1]), F32)
    for c in range(x_ref.shape[0] // ch):
        x1, x2, x3 = _split3(x_ref[c * ch:(c + 1) * ch, :])
        y = (jnp.dot(ltri, x1, preferred_element_type=F32)
             + jnp.dot(ltri, x2, preferred_element_type=F32)
             + jnp.dot(ltri, x3, preferred_element_type=F32)) + carry
        o_ref[c * ch:(c + 1) * ch, :] = y
        carry = y[ch - 1:ch, :]


def _cumsum_col_call(x, seq, ch):
    n, nh = x.shape
    tri = (jnp.arange(ch)[:, None] >= jnp.arange(ch)[None, :]).astype(BF16)
    return pl.pallas_call(
        functools.partial(_cumsum_col_kernel, ch=ch),
        grid=(n // seq,),
        in_specs=[pl.BlockSpec((seq, nh), lambda i: (i, 0)),
                  pl.BlockSpec((ch, ch), lambda i: (0, 0))],
        out_specs=pl.BlockSpec((seq, nh), lambda i: (i, 0)),
        out_shape=jax.ShapeDtypeStruct((n, nh), F32),
        compiler_params=_cparams(1),
        name="cumsum_col",
    )(x, tri)


def _lambda(lamv_ref, lam_init):
    lv = lamv_ref[...]
    return (jnp.exp(jnp.sum(lv[0:1] * lv[1:2], axis=1, keepdims=True))
            - jnp.exp(jnp.sum(lv[2:3] * lv[3:4], axis=1, keepdims=True)) + lam_init)


def _gate(g):
    g = g.astype(F32)
    return g / (1.0 + jnp.exp(-g))


def _stack_maps(q):
    lane = lax.broadcasted_iota(jnp.int32, q.shape, 1)
    zero = jnp.zeros_like(q)
    return jnp.concatenate([jnp.where(lane < HEAD // 2, q, zero),
                            jnp.where(lane >= HEAD // 2, q, zero)], axis=0)


def _self_attn_kernel(*refs, mode, tq, lam_init):
    it = iter(refs)
    q_ref, k_ref, v_ref, g_ref, tile_ref = next(it), next(it), next(it), next(it), next(it)
    if mode == "diff":
        slope_ref, lamv_ref, subg_ref = next(it), next(it), next(it)
    else:
        c_ref = next(it)
    o_ref = next(it)
    kaug_ref, vt_ref, m_ref, l_ref, acc_ref, sa_ref, sb_ref = (next(it) for _ in range(7))

    h = pl.program_id(1)
    i = pl.program_id(2)
    nkv = vt_ref.shape[0]
    tk = tq
    if mode == "diff":
        bscale = slope_ref[h] * LOG2E

    @pl.when(i == 0)
    def _():
        lane = lax.broadcasted_iota(jnp.int32, (tk, HEAD), 1)
        for j in range(nkv):
            rows = slice(j * tk, (j + 1) * tk)
            if mode == "diff":
                pos = lax.broadcasted_iota(jnp.int32, (tk, 1), 0) + j * tk
                bias = bscale * pos.astype(F32)
            else:
                c8 = c_ref[0, rows, :]
                sel = lax.broadcasted_iota(jnp.int32, c8.shape, 1) == h
                bias = -LOG2E * jnp.sum(jnp.where(sel, c8, 0.0), axis=1, keepdims=True)
            b1, b2, b3 = (piece.astype(F32) for piece in _split3(bias))
            extra = jnp.where(lane == 0, b1, jnp.where(lane == 1, b2,
                                                       jnp.where(lane == 2, b3, 0.0)))
            kaug_ref[rows, :HEAD] = k_ref[0, rows, :].astype(BF16)
            kaug_ref[rows, HEAD:] = extra.astype(BF16)
            vt_ref[j] = v_ref[0, rows, :].T.astype(BF16)

    q = q_ref[0]
    qs = _stack_maps(q) if mode == "diff" else q
    n = qs.shape[0]
    qlane = lax.broadcasted_iota(jnp.int32, (n, HEAD), 1)
    ones = jnp.where(qlane < N_BIAS, 1.0, 0.0).astype(BF16)
    qaug = jnp.concatenate([qs, ones], axis=1)

    m_ref[...] = jnp.full(m_ref.shape, NEG, F32)
    l_ref[...] = jnp.zeros(l_ref.shape, F32)
    acc_ref[...] = jnp.zeros(acc_ref.shape, F32)

    groups = [slice(c0, c0 + QCOLS) for c0 in range(0, n, QCOLS)]

    def scores(j, cols):
        k0 = pl.multiple_of(j * tk, tk)
        return lax.dot_general(kaug_ref[pl.ds(k0, tk), :], qaug[cols], _NT,
                               preferred_element_type=F32)

    def consume(s_ref, j, cols):
        s = s_ref[:, cols]
        m_prev = m_ref[:, cols]
        m_new = jnp.maximum(m_prev, jnp.max(s, axis=0, keepdims=True))
        alpha = jnp.exp2(m_prev - m_new)
        p = jnp.exp2(s - m_new)
        l_ref[:, cols] = alpha * l_ref[:, cols] + jnp.sum(p, axis=0, keepdims=True)
        acc_ref[:, cols] = alpha * acc_ref[:, cols] + jnp.dot(
            vt_ref[j], p.astype(BF16), preferred_element_type=F32)
        m_ref[:, cols] = m_new

    tile = tile_ref[...]
    if mode == "diff":
        tile = bscale * tile
    for cols in groups:
        t0 = cols.start % tq
        sa_ref[:, cols] = scores(i, cols) + tile[:, t0:t0 + QCOLS]

    def pipe_step(t, cur_ref, nxt_ref):
        jprev = jnp.where(t == 0, i, t - 1)
        for cols in groups:
            nxt_ref[:, cols] = scores(t, cols)
            consume(cur_ref, jprev, cols)

    def pipe_pair(pair, carry):
        pipe_step(2 * pair, sa_ref, sb_ref)
        pipe_step(2 * pair + 1, sb_ref, sa_ref)
        return carry

    lax.fori_loop(0, i // 2, pipe_pair, 0)
    jlast = jnp.maximum(i - 1, 0)

    @pl.when(i % 2 == 1)
    def _():
        pipe_step(i - 1, sa_ref, sb_ref)
        for cols in groups:
            consume(sb_ref, jlast, cols)

    @pl.when(i % 2 == 0)
    def _():
        for cols in groups:
            consume(sa_ref, jlast, cols)

    ot = acc_ref[...] * (1.0 / l_ref[...])
    if mode == "diff":
        ot = ot[:, :tq] - _lambda(lamv_ref, lam_init) * ot[:, tq:]
    o = ot.T
    if mode == "diff":
        o = o * _rms_scale(o) * subg_ref[...] * (1.0 - lam_init)
    o_ref[0] = (_gate(g_ref[0]) * o).astype(o_ref.dtype)


def _self_attn_call(mode, q, k, v, g, tile, extra, *, tq, lam_init=0.0):
    b, t, w = q.shape
    nh = w // HEAD
    n = 2 * tq if mode == "diff" else tq
    qmap = lambda bi, hi, ii: (bi, ii, hi)
    kmap = lambda bi, hi, ii: (bi, 0, hi)
    const2 = lambda bi, hi, ii: (0, 0)
    in_specs = [pl.BlockSpec((1, tq, HEAD), qmap),
                pl.BlockSpec((1, t, HEAD), kmap),
                pl.BlockSpec((1, t, HEAD), kmap),
                pl.BlockSpec((1, tq, HEAD), qmap),
                pl.BlockSpec((tq, tq), const2)]
    args = [q, k, v, g, tile]
    if mode == "diff":
        slopes, lamv, subg = extra
        in_specs += [pl.BlockSpec(memory_space=pltpu.SMEM),
                     pl.BlockSpec((4, LANES), const2),
                     pl.BlockSpec((1, HEAD), const2)]
        args += [slopes, lamv, subg]
    else:
        c = extra
        in_specs += [pl.BlockSpec((1, t, c.shape[2]), lambda bi, hi, ii: (bi, 0, 0))]
        args += [c]
    return pl.pallas_call(
        functools.partial(_self_attn_kernel, mode=mode, tq=tq, lam_init=lam_init),
        grid=(b, nh, t // tq),
        in_specs=in_specs,
        out_specs=pl.BlockSpec((1, tq, HEAD), qmap),
        out_shape=jax.ShapeDtypeStruct((b, t, w), BF16),
        scratch_shapes=[pltpu.VMEM((t, 2 * HEAD), BF16), pltpu.VMEM((t // tq, HEAD, tq), BF16),
                        pltpu.VMEM((1, n), F32), pltpu.VMEM((1, n), F32),
                        pltpu.VMEM((HEAD, n), F32),
                        pltpu.VMEM((tq, n), F32), pltpu.VMEM((tq, n), F32)],
        compiler_params=_cparams(3),
        name=f"attn_{mode}_self",
    )(*args)


def _cache_attn_kernel(*refs, mode, nh, lp, lam_init):
    it = iter(refs)
    q_ref, kc_ref, vc_ref, kn_ref, vn_ref, g_ref, tile_ref = (next(it) for _ in range(7))
    if mode == "diff":
        slope_ref, lamv_ref, subg_ref = next(it), next(it), next(it)
    else:
        cp_ref, cd_ref = next(it), next(it)
    o_ref = next(it)
    tq = q_ref.shape[1]
    tile = tile_ref[...]
    col = lax.broadcasted_iota(jnp.int32, (1, lp), 1).astype(F32)

    for h in range(nh):
        lanes = slice(h * HEAD, (h + 1) * HEAD)
        heads = pl.ds(h, lp, stride=nh)
        q = q_ref[0, :, lanes]
        kn = kn_ref[0, :, lanes].astype(BF16)
        vn = vn_ref[0, :, lanes].astype(BF16)
        vc = vc_ref[0, heads, :].astype(BF16)
        if mode == "diff":
            qs = _stack_maps(q)
            kc = kc_ref[0, :, lanes].astype(BF16)
            bscale = slope_ref[h] * LOG2E
            bias_c = bscale * (col - float(lp))
            bias_n = bscale * jnp.concatenate([tile, tile], axis=0)
        else:
            qs = q
            kc = kc_ref[0, heads, :].astype(BF16)
            cn = cd_ref[0, h:h + 1, :]
            cref = cn[:, 0:1]
            bias_c = LOG2E * (cref - cp_ref[0, h:h + 1, :])
            bias_n = tile + LOG2E * (cref - cn)
        s_c = lax.dot_general(qs, kc, _NT, preferred_element_type=F32) + bias_c
        s_n = lax.dot_general(qs, kn, _NT, preferred_element_type=F32) + bias_n
        m = jnp.maximum(jnp.max(s_c, axis=1, keepdims=True), jnp.max(s_n, axis=1, keepdims=True))
        p_c = jnp.exp2(s_c - m)
        p_n = jnp.exp2(s_n - m)
        l = jnp.sum(p_c, axis=1, keepdims=True) + jnp.sum(p_n, axis=1, keepdims=True)
        acc = (jnp.dot(p_c.astype(BF16), vc, preferred_element_type=F32)
               + jnp.dot(p_n.astype(BF16), vn, preferred_element_type=F32))
        o = acc / l
        if mode == "diff":
            o = o[:tq] - _lambda(lamv_ref, lam_init) * o[tq:]
            o = o * _rms_scale(o) * subg_ref[...] * (1.0 - lam_init)
        o_ref[0, :, lanes] = (_gate(g_ref[0, :, lanes]) * o).astype(o_ref.dtype)


def _cache_attn_call(mode, q, kc, vc, kn, vn, g, tile, extra, *, nh, lp, lam_init=0.0):
    b, tq, w = q.shape
    row3 = lambda bi: (bi, 0, 0)
    const2 = lambda bi: (0, 0)
    new_spec = pl.BlockSpec((1, tq, w), row3)
    in_specs = [new_spec,
                pl.BlockSpec((1,) + kc.shape[1:], row3),
                pl.BlockSpec((1,) + vc.shape[1:], row3),
                new_spec, new_spec, new_spec,
                pl.BlockSpec((tq, tq), const2)]
    args = [q, kc, vc, kn, vn, g, tile]
    if mode == "diff":
        slopes, lamv, subg = extra
        in_specs += [pl.BlockSpec(memory_space=pltpu.SMEM),
                     pl.BlockSpec((4, LANES), const2),
                     pl.BlockSpec((1, HEAD), const2)]
        args += [slopes, lamv, subg]
    else:
        cp, cd = extra
        in_specs += [pl.BlockSpec((1,) + cp.shape[1:], row3),
                     pl.BlockSpec((1,) + cd.shape[1:], row3)]
        args += [cp, cd]
    return pl.pallas_call(
        functools.partial(_cache_attn_kernel, mode=mode, nh=nh, lp=lp, lam_init=lam_init),
        grid=(b,),
        in_specs=in_specs,
        out_specs=new_spec,
        out_shape=jax.ShapeDtypeStruct((b, tq, w), BF16),
        compiler_params=_cparams(1),
        name=f"attn_{mode}_cache",
    )(*args)


def _out_kernel(x_ref, ma_ref, mb_ref, wa_ref, wb_ref, gf_ref, o_ref):
    hres = (x_ref[...]
            + jnp.dot(ma_ref[...], wa_ref[...], preferred_element_type=F32)
            + jnp.dot(mb_ref[...], wb_ref[...], preferred_element_type=F32))
    o_ref[...] = hres * _rms_scale(hres) * gf_ref[...]


def _out_call(x2, ma, mb, wa, wb, gf, tm):
    m, d = x2.shape
    da, db = ma.shape[1], mb.shape[1]
    return pl.pallas_call(
        _out_kernel,
        grid=(m // tm,),
        in_specs=[pl.BlockSpec((tm, d), lambda i: (i, 0)),
                  pl.BlockSpec((tm, da), lambda i: (i, 0)),
                  pl.BlockSpec((tm, db), lambda i: (i, 0)),
                  pl.BlockSpec((da, d), lambda i: (0, 0)),
                  pl.BlockSpec((db, d), lambda i: (0, 0)),
                  pl.BlockSpec((1, d), lambda i: (0, 0))],
        out_specs=pl.BlockSpec((tm, d), lambda i: (i, 0)),
        out_shape=jax.ShapeDtypeStruct((m, d), F32),
        compiler_params=_cparams(1),
        name="out_proj",
    )(x2, ma, mb, wa, wb, gf)


def _chunk_allowed(q, k):
    return (k // CHUNK) <= (q // CHUNK)


def _diff_tile_kq(t):
    k = jnp.arange(t)[:, None]
    q = jnp.arange(t)[None, :]
    return jnp.where(_chunk_allowed(q, k), jnp.minimum(0, 2 * (q - k)).astype(F32), NEG)


def _causal_tile_kq(t):
    k = jnp.arange(t)[:, None]
    q = jnp.arange(t)[None, :]
    return jnp.where(k <= q, 0.0, NEG).astype(F32)


def _diff_tile_qk(t):
    q = jnp.arange(t)[:, None]
    k = jnp.arange(t)[None, :]
    return jnp.where(_chunk_allowed(q, k), (q - jnp.abs(q - k)).astype(F32), NEG)


def _causal_tile_qk(t):
    return _causal_tile_kq(t).T


def _layer(x, past, params, *, tm, tq, cum_ch):
    (g_norm, w_bf, wf_hi, wf_lo, b_pad, slopes, lamv, subg, lam_init,
     wo_a, wo_b, g_final, d_a, nh_b) = params
    b, s, d = x.shape
    nh_a = d_a // HEAD
    x2 = x.reshape(b * s, d)
    q_scale_a = float((HEAD // 2) ** -0.5 * LOG2E)
    q_scale_b = float(HEAD ** -0.5 * LOG2E)

    proj = lambda slab, scale, dt: _proj_call(x2, g_norm, w_bf, slab, d_a, scale, dt, tm)
    qa = proj(0, q_scale_a, BF16)
    ka = proj(1, 1.0, F32)
    va = proj(2, 1.0, F32)
    ga = proj(3, 1.0, BF16)
    qb = proj(4, q_scale_b, BF16)
    kb = proj(5, 1.0, F32)
    vb = proj(6, 1.0, F32)
    gb = proj(7, 1.0, BF16)
    logf, logf_t = _logf_call(x2, g_norm, wf_hi, wf_lo, b_pad, nh_b, tm)

    r3 = lambda a: a.reshape(b, s, a.shape[-1])
    if past is None:
        c = _cumsum_col_call(logf, s, cum_ch).reshape(b, s, nh_b)
        ma = _self_attn_call("diff", r3(qa), r3(ka), r3(va), r3(ga), _diff_tile_kq(tq),
                             (slopes, lamv, subg), tq=tq, lam_init=lam_init)
        mb = _self_attn_call("fox", r3(qb), r3(kb), r3(vb), r3(gb), _causal_tile_kq(tq), c, tq=tq)
    else:
        cak, cav, cbk, cbv, cbf = past
        lp = cak.shape[1]
        assert lp % CHUNK == 0 and s == tq <= CHUNK
        tot = lp + s
        pad = (-tot) % cum_ch
        lf_all = jnp.concatenate(
            [jnp.transpose(cbf, (2, 0, 1)), logf_t.reshape(nh_b, b, s),
             jnp.zeros((nh_b, b, pad), F32)], axis=2)
        c = _cumsum_row_call(lf_all.reshape(nh_b, b * (tot + pad)), cum_ch, (tot + pad) // cum_ch)
        c = jnp.transpose(c.reshape(nh_b, b, tot + pad), (1, 0, 2))
        slab = lambda a: a.reshape(b, lp * a.shape[2], HEAD)
        ma = _cache_attn_call("diff", r3(qa), cak.reshape(b, lp, d_a), slab(cav), r3(ka), r3(va),
                              r3(ga), _diff_tile_qk(tq), (slopes, lamv, subg),
                              nh=nh_a, lp=lp, lam_init=lam_init)
        mb = _cache_attn_call("fox", r3(qb), slab(cbk), slab(cbv), r3(kb), r3(vb), r3(gb),
                              _causal_tile_qk(tq), (c[:, :, :lp], c[:, :, lp:tot]),
                              nh=nh_b, lp=lp)

    y = _out_call(x2, ma.reshape(b * s, -1), mb.reshape(b * s, -1), wo_a, wo_b, g_final,
                  min(tm, 512))
    return y.reshape(b, s, d), ka, va, kb, vb, logf


def kernel(x_prompt, x_sample, cache_a_k, cache_a_v, cache_b_k, cache_b_v, cache_b_logf,
           g_norm, w_in, b_f, lam_q1, lam_k1, lam_q2, lam_k2, sub_gain, w_out, g_final):
    depth, d = g_norm.shape
    assert depth == 1
    nh_a2, hqk = cache_a_k.shape[3], cache_a_k.shape[4]
    nh_a = nh_a2 // 2
    nh_b = cache_b_k.shape[3]
    d_a = nh_a2 * hqk
    assert 2 * hqk == HEAD and cache_b_k.shape[4] == HEAD and d == 2 * d_a
    lyr = 0
    lam_init = 0.8 - 0.6 * math.exp(-0.3 * lyr)

    w = w_in[lyr]
    w_bf = w.astype(BF16)
    wf = jnp.pad(w[:, 8 * d_a:], ((0, 0), (0, LANES - nh_b)))
    wf_hi = wf.astype(BF16)
    wf_lo = (wf - wf_hi.astype(F32)).astype(BF16)
    b_pad = jnp.pad(b_f[lyr].astype(F32), (0, LANES - nh_b)).reshape(1, LANES)
    slopes = 2.0 ** (-8.0 * jnp.arange(1, nh_a + 1, dtype=F32) / nh_a)
    lamv = jnp.pad(jnp.stack([lam_q1[lyr], lam_k1[lyr], lam_q2[lyr], lam_k2[lyr]]).astype(F32),
                   ((0, 0), (0, LANES - hqk)))
    subg = sub_gain[lyr].astype(F32).reshape(1, HEAD)
    wo = w_out[lyr].astype(BF16)
    params = (g_norm[lyr].reshape(1, d), w_bf, wf_hi, wf_lo, b_pad, slopes, lamv, subg, lam_init,
              wo[:d_a], wo[d_a:], g_final.reshape(1, d), d_a, nh_b)

    bp, sp, _ = x_prompt.shape
    bs, ss, _ = x_sample.shape
    yp, pak, pav, pbk, pbv, pbf = _layer(x_prompt, None, params, tm=1024, tq=512, cum_ch=256)
    past = (cache_a_k[lyr], cache_a_v[lyr], cache_b_k[lyr], cache_b_v[lyr], cache_b_logf[lyr])
    ys, sak, sav, sbk, sbv, sbf = _layer(x_sample, past, params, tm=bs * ss, tq=ss, cum_ch=128)

    shp = lambda a, bb, s_, n, e: a.reshape(1, bb, s_, n, e)
    return (yp, ys,
            shp(pak, bp, sp, nh_a2, hqk), shp(pav, bp, sp, nh_a, HEAD),
            shp(pbk, bp, sp, nh_b, HEAD), shp(pbv, bp, sp, nh_b, HEAD),
            pbf.reshape(1, bp, sp, nh_b),
            shp(sak, bs, ss, nh_a2, hqk), shp(sav, bs, ss, nh_a, HEAD),
            shp(sbk, bs, ss, nh_b, HEAD), shp(sbv, bs, ss, nh_b, HEAD),
            sbf.reshape(1, bs, ss, nh_b))
```

```python
import functools
import math

import jax
import jax.numpy as jnp
from jax import lax
from jax.experimental import pallas as pl
from jax.experimental.pallas import tpu as pltpu

F32 = jnp.float32
BF16 = jnp.bfloat16

EPS = 1e-6
CHUNK = 64
HEAD = 128
NEG = -1e30
LANES = 128
LOG2E = math.log2(math.e)
N_BIAS = 3
QCOLS = 256
VMEM_LIMIT = 56 * 1024 * 1024

_NT = (((1,), (1,)), ((), ()))


def _cparams(n_axes):
    return pltpu.CompilerParams(
        dimension_semantics=("arbitrary",) * n_axes, vmem_limit_bytes=VMEM_LIMIT)


def _rms_scale(x):
    return lax.rsqrt(jnp.mean(x * x, axis=-1, keepdims=True) + EPS)


def _split3(x):
    x1 = x.astype(BF16)
    r1 = x - x1.astype(F32)
    x2 = r1.astype(BF16)
    x3 = (r1 - x2.astype(F32)).astype(BF16)
    return x1, x2, x3


def _proj_kernel(x_ref, g_ref, w_ref, o_ref, *, scale):
    x = x_ref[...]
    xn = (x * _rms_scale(x) * g_ref[...]).astype(BF16)
    z = jnp.dot(xn, w_ref[...], preferred_element_type=F32)
    if scale != 1.0:
        z = z * scale
    o_ref[...] = z.astype(o_ref.dtype)


def _proj_call(x2, g, w_bf, slab, width, scale, out_dtype, tm):
    m, d = x2.shape
    return pl.pallas_call(
        functools.partial(_proj_kernel, scale=scale),
        grid=(m // tm,),
        in_specs=[pl.BlockSpec((tm, d), lambda i: (i, 0)),
                  pl.BlockSpec((1, d), lambda i: (0, 0)),
                  pl.BlockSpec((d, width), lambda i: (0, slab))],
        out_specs=pl.BlockSpec((tm, width), lambda i: (i, 0)),
        out_shape=jax.ShapeDtypeStruct((m, width), out_dtype),
        compiler_params=_cparams(1),
        name=f"proj_slab{slab}",
    )(x2, g, w_bf)


def _logf_kernel(x_ref, g_ref, whi_ref, wlo_ref, b_ref, o_ref, *, nh):
    x = x_ref[...]
    xn = x * _rms_scale(x) * g_ref[...]
    hi = xn.astype(BF16)
    lo = (xn - hi.astype(F32)).astype(BF16)
    whi = whi_ref[...]
    fl = (jnp.dot(hi, whi, preferred_element_type=F32)
          + jnp.dot(hi, wlo_ref[...], preferred_element_type=F32)
          + jnp.dot(lo, whi, preferred_element_type=F32))
    t = fl + b_ref[...]
    logf = jnp.minimum(t, 0.0) - jnp.log(1.0 + jnp.exp(-jnp.abs(t)))
    o_ref[...] = logf[:, :nh]


def _logf_call(x2, g, wf_hi, wf_lo, b_pad, nh, tm):
    m, d = x2.shape
    return pl.pallas_call(
        functools.partial(_logf_kernel, nh=nh),
        grid=(m // tm,),
        in_specs=[pl.BlockSpec((tm, d), lambda i: (i, 0)),
                  pl.BlockSpec((1, d), lambda i: (0, 0)),
                  pl.BlockSpec((d, LANES), lambda i: (0, 0)),
                  pl.BlockSpec((d, LANES), lambda i: (0, 0)),
                  pl.BlockSpec((1, LANES), lambda i: (0, 0))],
        out_specs=pl.BlockSpec((tm, nh), lambda i: (i, 0)),
        out_shape=jax.ShapeDtypeStruct((m, nh), F32),
        compiler_params=_cparams(1),
        name="logf",
    )(x2, g, wf_hi, wf_lo, b_pad)


def _cumsum_kernel(x_ref, l_ref, o_ref, *, ch):
    ltri = l_ref[...]
    carry = jnp.zeros((1, x_ref.shape[1]), F32)
    for c in range(x_ref.shape[0] // ch):
        x1, x2, x3 = _split3(x_ref[c * ch:(c + 1) * ch, :])
        y = (jnp.dot(ltri, x1, preferred_element_type=F32)
             + jnp.dot(ltri, x2, preferred_element_type=F32)
             + jnp.dot(ltri, x3, preferred_element_type=F32)) + carry
        o_ref[c * ch:(c + 1) * ch, :] = y
        carry = y[ch - 1:ch, :]


def _cumsum_call(x, seq, ch):
    n, nh = x.shape
    tri = (jnp.arange(ch)[:, None] >= jnp.arange(ch)[None, :]).astype(BF16)
    return pl.pallas_call(
        functools.partial(_cumsum_kernel, ch=ch),
        grid=(n // seq,),
        in_specs=[pl.BlockSpec((seq, nh), lambda i: (i, 0)),
                  pl.BlockSpec((ch, ch), lambda i: (0, 0))],
        out_specs=pl.BlockSpec((seq, nh), lambda i: (i, 0)),
        out_shape=jax.ShapeDtypeStruct((n, nh), F32),
        compiler_params=_cparams(1),
        name="cumsum",
    )(x, tri)


def _lambda(lamv_ref, lam_init):
    lv = lamv_ref[...]
    return (jnp.exp(jnp.sum(lv[0:1] * lv[1:2], axis=1, keepdims=True))
            - jnp.exp(jnp.sum(lv[2:3] * lv[3:4], axis=1, keepdims=True)) + lam_init)


def _gate(g):
    g = g.astype(F32)
    return g / (1.0 + jnp.exp(-g))


def _stack_maps(q):
    lane = lax.broadcasted_iota(jnp.int32, q.shape, 1)
    zero = jnp.zeros_like(q)
    return jnp.concatenate([jnp.where(lane < HEAD // 2, q, zero),
                            jnp.where(lane >= HEAD // 2, q, zero)], axis=0)


def _self_attn_kernel(*refs, mode, tq, lam_init):
    it = iter(refs)
    q_ref, k_ref, v_ref, g_ref, tile_ref = next(it), next(it), next(it), next(it), next(it)
    if mode == "diff":
        slope_ref, lamv_ref, subg_ref = next(it), next(it), next(it)
    else:
        c_ref = next(it)
    o_ref = next(it)
    (kaug_ref, vt_ref, qaug_ref, m_ref, l_ref, acc_ref, sa_ref, sb_ref) = (
        next(it) for _ in range(8))

    h = pl.program_id(1)
    pi = pl.program_id(2)
    nq = vt_ref.shape[0]
    tk = tq
    n = qaug_ref.shape[1]
    blocks = (pi, nq - 1 - pi)
    if mode == "diff":
        bscale = slope_ref[h] * LOG2E

    @pl.when(pi == 0)
    def _():
        lane = lax.broadcasted_iota(jnp.int32, (tk, HEAD), 1)
        for j in range(nq):
            rows = slice(j * tk, (j + 1) * tk)
            if mode == "diff":
                pos = lax.broadcasted_iota(jnp.int32, (tk, 1), 0) + j * tk
                bias = bscale * pos.astype(F32)
            else:
                c8 = c_ref[0, rows, :]
                sel = lax.broadcasted_iota(jnp.int32, c8.shape, 1) == h
                bias = -LOG2E * jnp.sum(jnp.where(sel, c8, 0.0), axis=1, keepdims=True)
            b1, b2, b3 = (piece.astype(F32) for piece in _split3(bias))
            extra = jnp.where(lane == 0, b1, jnp.where(lane == 1, b2,
                                                       jnp.where(lane == 2, b3, 0.0)))
            kaug_ref[rows, :HEAD] = k_ref[0, rows, :].astype(BF16)
            kaug_ref[rows, HEAD:] = extra.astype(BF16)
            vt_ref[j] = v_ref[0, rows, :].T.astype(BF16)

    qlane = lax.broadcasted_iota(jnp.int32, (n, HEAD), 1)
    ones = jnp.where(qlane < N_BIAS, 1.0, 0.0).astype(BF16)
    for side, blk in enumerate(blocks):
        q = q_ref[0, pl.ds(pl.multiple_of(blk * tq, tq), tq), :]
        qs = _stack_maps(q) if mode == "diff" else q
        qaug_ref[side] = jnp.concatenate([qs, ones], axis=1)
    m_ref[...] = jnp.full(m_ref.shape, NEG, F32)
    l_ref[...] = jnp.zeros(l_ref.shape, F32)
    acc_ref[...] = jnp.zeros(acc_ref.shape, F32)

    groups = [slice(c0, c0 + QCOLS) for c0 in range(0, n, QCOLS)]

    def product(s):
        if isinstance(s, int) and s == 0:
            return 0, blocks[0]
        if isinstance(s, int) and s == nq:
            return 1, blocks[1]
        first = s <= blocks[0]
        past = jnp.where(first, s - 1, s - 1 - blocks[0])
        return jnp.where(first, 0, 1), jnp.where(s == 0, blocks[0], past)

    def scores(s, cols):
        side, j = product(s)
        k0 = pl.multiple_of(j * tk, tk)
        return lax.dot_general(kaug_ref[pl.ds(k0, tk), :], qaug_ref[side, cols, :], _NT,
                               preferred_element_type=F32)

    def consume(s_ref, s, cols):
        side, j = product(s)
        sc = s_ref[:, cols]
        m_prev = m_ref[side, :, cols]
        m_new = jnp.maximum(m_prev, jnp.max(sc, axis=0, keepdims=True))
        alpha = jnp.exp2(m_prev - m_new)
        p = jnp.exp2(sc - m_new)
        l_ref[side, :, cols] = alpha * l_ref[side, :, cols] + jnp.sum(p, axis=0, keepdims=True)
        acc_ref[side, :, cols] = alpha * acc_ref[side, :, cols] + jnp.dot(
            vt_ref[j], p.astype(BF16), preferred_element_type=F32)
        m_ref[side, :, cols] = m_new

    tile = tile_ref[...]
    if mode == "diff":
        tile = bscale * tile

    def pipe_step(s, cur_ref, nxt_ref, diagonal=False):
        for cols in groups:
            sc = scores(s, cols)
            if diagonal:
                t0 = cols.start % tq
                sc = sc + tile[:, t0:t0 + QCOLS]
            nxt_ref[:, cols] = sc
            if cur_ref is not None:
                consume(cur_ref, s - 1, cols)

    pipe_step(0, None, sa_ref, diagonal=True)

    def pipe_pair(pair, carry):
        pipe_step(2 * pair + 1, sa_ref, sb_ref)
        pipe_step(2 * pair + 2, sb_ref, sa_ref)
        return carry

    lax.fori_loop(0, (nq - 2) // 2, pipe_pair, 0)
    pipe_step(nq - 1, sa_ref, sb_ref)
    pipe_step(nq, sb_ref, sa_ref, diagonal=True)
    for cols in groups:
        consume(sa_ref, nq, cols)

    for side, blk in enumerate(blocks):
        rows = pl.ds(pl.multiple_of(blk * tq, tq), tq)
        ot = acc_ref[side] * (1.0 / l_ref[side])
        if mode == "diff":
            ot = ot[:, :tq] - _lambda(lamv_ref, lam_init) * ot[:, tq:]
        o = ot.T
        if mode == "diff":
            o = o * _rms_scale(o) * subg_ref[...] * (1.0 - lam_init)
        o_ref[0, rows, :] = (_gate(g_ref[0, rows, :]) * o).astype(o_ref.dtype)


def _self_attn_call(mode, q, k, v, g, tile, extra, *, tq, lam_init=0.0):
    b, t, w = q.shape
    nh = w // HEAD
    nq = t // tq
    assert nq % 2 == 0
    n = 2 * tq if mode == "diff" else tq
    seq_spec = pl.BlockSpec((1, t, HEAD), lambda bi, hi, pi: (bi, 0, hi))
    const2 = lambda bi, hi, pi: (0, 0)
    in_specs = [seq_spec, seq_spec, seq_spec, seq_spec, pl.BlockSpec((tq, tq), const2)]
    args = [q, k, v, g, tile]
    if mode == "diff":
        slopes, lamv, subg = extra
        in_specs += [pl.BlockSpec(memory_space=pltpu.SMEM),
                     pl.BlockSpec((4, LANES), const2),
                     pl.BlockSpec((1, HEAD), const2)]
        args += [slopes, lamv, subg]
    else:
        c = extra
        in_specs += [pl.BlockSpec((1, t, c.shape[2]), lambda bi, hi, pi: (bi, 0, 0))]
        args += [c]
    return pl.pallas_call(
        functools.partial(_self_attn_kernel, mode=mode, tq=tq, lam_init=lam_init),
        grid=(b, nh, nq // 2),
        in_specs=in_specs,
        out_specs=seq_spec,
        out_shape=jax.ShapeDtypeStruct((b, t, w), BF16),
        scratch_shapes=[pltpu.VMEM((t, 2 * HEAD), BF16), pltpu.VMEM((nq, HEAD, tq), BF16),
                        pltpu.VMEM((2, n, 2 * HEAD), BF16),
                        pltpu.VMEM((2, 1, n), F32), pltpu.VMEM((2, 1, n), F32),
                        pltpu.VMEM((2, HEAD, n), F32),
                        pltpu.VMEM((tq, n), F32), pltpu.VMEM((tq, n), F32)],
        compiler_params=_cparams(3),
        name=f"attn_{mode}_self",
    )(*args)


def _cache_attn_kernel(*refs, mode, nh, lp, lam_init):
    it = iter(refs)
    q_ref, kc_ref, vc_ref, kn_ref, vn_ref, g_ref, tile_ref = (next(it) for _ in range(7))
    if mode == "diff":
        slope_ref, lamv_ref, subg_ref = next(it), next(it), next(it)
    else:
        cp_ref, cd_ref = next(it), next(it)
    o_ref = next(it)
    tq = q_ref.shape[1]
    tile = tile_ref[...]
    col = lax.broadcasted_iota(jnp.int32, (1, lp), 1).astype(F32)

    for h in range(nh):
        lanes = slice(h * HEAD, (h + 1) * HEAD)
        heads = pl.ds(h, lp, stride=nh)
        q = q_ref[0, :, lanes]
        kn = kn_ref[0, :, lanes].astype(BF16)
        vn = vn_ref[0, :, lanes].astype(BF16)
        vc = vc_ref[0, heads, :].astype(BF16)
        if mode == "diff":
            qs = _stack_maps(q)
            kc = kc_ref[0, :, lanes].astype(BF16)
            bscale = slope_ref[h] * LOG2E
            bias_c = bscale * (col - float(lp))
            bias_n = bscale * jnp.concatenate([tile, tile], axis=0)
        else:
            qs = q
            kc = kc_ref[0, heads, :].astype(BF16)
            cn = cd_ref[0, h:h + 1, :]
            cref = cn[:, 0:1]
            bias_c = LOG2E * (cref - cp_ref[0, h:h + 1, :])
            bias_n = tile + LOG2E * (cref - cn)
        s_c = lax.dot_general(qs, kc, _NT, preferred_element_type=F32) + bias_c
        s_n = lax.dot_general(qs, kn, _NT, preferred_element_type=F32) + bias_n
        m = jnp.maximum(jnp.max(s_c, axis=1, keepdims=True), jnp.max(s_n, axis=1, keepdims=True))
        p_c = jnp.exp2(s_c - m)
        p_n = jnp.exp2(s_n - m)
        l = jnp.sum(p_c, axis=1, keepdims=True) + jnp.sum(p_n, axis=1, keepdims=True)
        acc = (jnp.dot(p_c.astype(BF16), vc, preferred_element_type=F32)
               + jnp.dot(p_n.astype(BF16), vn, preferred_element_type=F32))
        o = acc / l
        if mode == "diff":
            o = o[:tq] - _lambda(lamv_ref, lam_init) * o[tq:]
            o = o * _rms_scale(o) * subg_ref[...] * (1.0 - lam_init)
        o_ref[0, :, lanes] = (_gate(g_ref[0, :, lanes]) * o).astype(o_ref.dtype)


def _cache_attn_call(mode, q, kc, vc, kn, vn, g, tile, extra, *, nh, lp, lam_init=0.0):
    b, tq, w = q.shape
    row3 = lambda bi: (bi, 0, 0)
    const2 = lambda bi: (0, 0)
    new_spec = pl.BlockSpec((1, tq, w), row3)
    in_specs = [new_spec,
                pl.BlockSpec((1,) + kc.shape[1:], row3),
                pl.BlockSpec((1,) + vc.shape[1:], row3),
                new_spec, new_spec, new_spec,
                pl.BlockSpec((tq, tq), const2)]
    args = [q, kc, vc, kn, vn, g, tile]
    if mode == "diff":
        slopes, lamv, subg = extra
        in_specs += [pl.BlockSpec(memory_space=pltpu.SMEM),
                     pl.BlockSpec((4, LANES), const2),
                     pl.BlockSpec((1, HEAD), const2)]
        args += [slopes, lamv, subg]
    else:
        cp, cd = extra
        in_specs += [pl.BlockSpec((1,) + cp.shape[1:], row3),
                     pl.BlockSpec((1,) + cd.shape[1:], row3)]
        args += [cp, cd]
    return pl.pallas_call(
        functools.partial(_cache_attn_kernel, mode=mode, nh=nh, lp=lp, lam_init=lam_init),
        grid=(b,),
        in_specs=in_specs,
        out_specs=new_spec,
        out_shape=jax.ShapeDtypeStruct((b, tq, w), BF16),
        compiler_params=_cparams(1),
        name=f"attn_{mode}_cache",
    )(*args)


def _out_kernel(x_ref, ma_ref, mb_ref, wa_ref, wb_ref, gf_ref, o_ref):
    hres = (x_ref[...]
            + jnp.dot(ma_ref[...], wa_ref[...], preferred_element_type=F32)
            + jnp.dot(mb_ref[...], wb_ref[...], preferred_element_type=F32))
    o_ref[...] = hres * _rms_scale(hres) * gf_ref[...]


def _out_call(x2, ma, mb, wa, wb, gf, tm):
    m, d = x2.shape
    da, db = ma.shape[1], mb.shape[1]
    return pl.pallas_call(
        _out_kernel,
        grid=(m // tm,),
        in_specs=[pl.BlockSpec((tm, d), lambda i: (i, 0)),
                  pl.BlockSpec((tm, da), lambda i: (i, 0)),
                  pl.BlockSpec((tm, db), lambda i: (i, 0)),
                  pl.BlockSpec((da, d), lambda i: (0, 0)),
                  pl.BlockSpec((db, d), lambda i: (0, 0)),
                  pl.BlockSpec((1, d), lambda i: (0, 0))],
        out_specs=pl.BlockSpec((tm, d), lambda i: (i, 0)),
        out_shape=jax.ShapeDtypeStruct((m, d), F32),
        compiler_params=_cparams(1),
        name="out_proj",
    )(x2, ma, mb, wa, wb, gf)


def _chunk_allowed(q, k):
    return (k // CHUNK) <= (q // CHUNK)


def _diff_tile_kq(t):
    k = jnp.arange(t)[:, None]
    q = jnp.arange(t)[None, :]
    return jnp.where(_chunk_allowed(q, k), jnp.minimum(0, 2 * (q - k)).astype(F32), NEG)


def _causal_tile_kq(t):
    k = jnp.arange(t)[:, None]
    q = jnp.arange(t)[None, :]
    return jnp.where(k <= q, 0.0, NEG).astype(F32)


def _diff_tile_qk(t):
    q = jnp.arange(t)[:, None]
    k = jnp.arange(t)[None, :]
    return jnp.where(_chunk_allowed(q, k), (q - jnp.abs(q - k)).astype(F32), NEG)


def _causal_tile_qk(t):
    return _causal_tile_kq(t).T


def _layer(x, past, params, *, tm, tq, cum_ch):
    (g_norm, w_bf, wf_hi, wf_lo, b_pad, slopes, lamv, subg, lam_init,
     wo_a, wo_b, g_final, d_a, nh_b) = params
    b, s, d = x.shape
    nh_a = d_a // HEAD
    x2 = x.reshape(b * s, d)
    q_scale_a = float((HEAD // 2) ** -0.5 * LOG2E)
    q_scale_b = float(HEAD ** -0.5 * LOG2E)

    proj = lambda slab, scale, dt: _proj_call(x2, g_norm, w_bf, slab, d_a, scale, dt, tm)
    qa = proj(0, q_scale_a, BF16)
    ka = proj(1, 1.0, F32)
    va = proj(2, 1.0, F32)
    ga = proj(3, 1.0, BF16)
    qb = proj(4, q_scale_b, BF16)
    kb = proj(5, 1.0, F32)
    vb = proj(6, 1.0, F32)
    gb = proj(7, 1.0, BF16)
    logf = _logf_call(x2, g_norm, wf_hi, wf_lo, b_pad, nh_b, tm)

    r3 = lambda a: a.reshape(b, s, a.shape[-1])
    if past is None:
        c = _cumsum_call(logf, s, cum_ch).reshape(b, s, nh_b)
        ma = _self_attn_call("diff", r3(qa), r3(ka), r3(va), r3(ga), _diff_tile_kq(tq),
                             (slopes, lamv, subg), tq=tq, lam_init=lam_init)
        mb = _self_attn_call("fox", r3(qb), r3(kb), r3(vb), r3(gb), _causal_tile_kq(tq), c, tq=tq)
    else:
        cak, cav, cbk, cbv, cbf = past
        lp = cak.shape[1]
        assert lp % CHUNK == 0 and s == tq <= CHUNK
        tot = lp + s
        padded = -(-tot // cum_ch) * cum_ch
        lf_all = jnp.concatenate([cbf, r3(logf), jnp.zeros((b, padded - tot, nh_b), F32)], axis=1)
        c = _cumsum_call(lf_all.reshape(b * padded, nh_b), padded, cum_ch)
        c = jnp.transpose(c.reshape(b, padded, nh_b), (0, 2, 1))
        slab = lambda a: a.reshape(b, lp * a.shape[2], HEAD)
        ma = _cache_attn_call("diff", r3(qa), cak.reshape(b, lp, d_a), slab(cav), r3(ka), r3(va),
                              r3(ga), _diff_tile_qk(tq), (slopes, lamv, subg),
                              nh=nh_a, lp=lp, lam_init=lam_init)
        mb = _cache_attn_call("fox", r3(qb), slab(cbk), slab(cbv), r3(kb), r3(vb), r3(gb),
                              _causal_tile_qk(tq), (c[:, :, :lp], c[:, :, lp:tot]),
                              nh=nh_b, lp=lp)

    y = _out_call(x2, ma.reshape(b * s, -1), mb.reshape(b * s, -1), wo_a, wo_b, g_final,
                  min(tm, 512))
    return y.reshape(b, s, d), ka, va, kb, vb, logf


def kernel(x_prompt, x_sample, cache_a_k, cache_a_v, cache_b_k, cache_b_v, cache_b_logf,
           g_norm, w_in, b_f, lam_q1, lam_k1, lam_q2, lam_k2, sub_gain, w_out, g_final):
    depth, d = g_norm.shape
    assert depth == 1
    nh_a2, hqk = cache_a_k.shape[3], cache_a_k.shape[4]
    nh_a = nh_a2 // 2
    nh_b = cache_b_k.shape[3]
    d_a = nh_a2 * hqk
    assert 2 * hqk == HEAD and cache_b_k.shape[4] == HEAD and d == 2 * d_a
    lyr = 0
    lam_init = 0.8 - 0.6 * math.exp(-0.3 * lyr)

    w = w_in[lyr]
    w_bf = w.astype(BF16)
    wf = jnp.pad(w[:, 8 * d_a:], ((0, 0), (0, LANES - nh_b)))
    wf_hi = wf.astype(BF16)
    wf_lo = (wf - wf_hi.astype(F32)).astype(BF16)
    b_pad = jnp.pad(b_f[lyr].astype(F32), (0, LANES - nh_b)).reshape(1, LANES)
    slopes = 2.0 ** (-8.0 * jnp.arange(1, nh_a + 1, dtype=F32) / nh_a)
    lamv = jnp.pad(jnp.stack([lam_q1[lyr], lam_k1[lyr], lam_q2[lyr], lam_k2[lyr]]).astype(F32),
                   ((0, 0), (0, LANES - hqk)))
    subg = sub_gain[lyr].astype(F32).reshape(1, HEAD)
    wo = w_out[lyr].astype(BF16)
    params = (g_norm[lyr].reshape(1, d), w_bf, wf_hi, wf_lo, b_pad, slopes, lamv, subg, lam_init,
              wo[:d_a], wo[d_a:], g_final.reshape(1, d), d_a, nh_b)

    bp, sp, _ = x_prompt.shape
    bs, ss, _ = x_sample.shape
    yp, pak, pav, pbk, pbv, pbf = _layer(x_prompt, None, params, tm=1024, tq=512, cum_ch=256)
    past = (cache_a_k[lyr], cache_a_v[lyr], cache_b_k[lyr], cache_b_v[lyr], cache_b_logf[lyr])
    ys, sak, sav, sbk, sbv, sbf = _layer(x_sample, past, params, tm=bs * ss, tq=ss, cum_ch=256)

    shp = lambda a, bb, s_, n, e: a.reshape(1, bb, s_, n, e)
    return (yp, ys,
            shp(pak, bp, sp, nh_a2, hqk), shp(pav, bp, sp, nh_a, HEAD),
            shp(pbk, bp, sp, nh_b, HEAD), shp(pbv, bp, sp, nh_b, HEAD),
            pbf.reshape(1, bp, sp, nh_b),
            shp(sak, bs, ss, nh_a2, hqk), shp(sav, bs, ss, nh_a, HEAD),
            shp(sbk, bs, ss, nh_b, HEAD), shp(sbv, bs, ss, nh_b, HEAD),
            sbf.reshape(1, bs, ss, nh_b))
```

```python
import functools
import math

import jax
import jax.numpy as jnp
from jax import lax
from jax.experimental import pallas as pl
from jax.experimental.pallas import tpu as pltpu

F32 = jnp.float32
BF16 = jnp.bfloat16

EPS = 1e-6
CHUNK = 64
HEAD = 128
NEG = -1e30
LANES = 128
LOG2E = math.log2(math.e)
N_BIAS = 3
QCOLS = 256
VMEM_LIMIT = 56 * 1024 * 1024

_NT = (((1,), (1,)), ((), ()))


def _cparams(n_axes):
    return pltpu.CompilerParams(
        dimension_semantics=("arbitrary",) * n_axes, vmem_limit_bytes=VMEM_LIMIT)


def _rms_scale(x):
    return lax.rsqrt(jnp.mean(x * x, axis=-1, keepdims=True) + EPS)


def _split3(x):
    x1 = x.astype(BF16)
    r1 = x - x1.astype(F32)
    x2 = r1.astype(BF16)
    x3 = (r1 - x2.astype(F32)).astype(BF16)
    return x1, x2, x3


def _proj_kernel(*refs, scales, width, nh_logf):
    x_ref, g_ref, w_ref = refs[:3]
    o_refs = refs[-len(scales) - (1 if nh_logf else 0):]
    x = x_ref[...]
    xn32 = x * _rms_scale(x) * g_ref[...]
    xn = xn32.astype(BF16)
    for j, (scale, o_ref) in enumerate(zip(scales, o_refs)):
        z = jnp.dot(xn, w_ref[:, j * width:(j + 1) * width], preferred_element_type=F32)
        if scale != 1.0:
            z = z * scale
        o_ref[...] = z.astype(o_ref.dtype)
    if nh_logf:
        whi_ref, wlo_ref, b_ref = refs[3:6]
        lo = (xn32 - xn.astype(F32)).astype(BF16)
        whi = whi_ref[...]
        fl = (jnp.dot(xn, whi, preferred_element_type=F32)
              + jnp.dot(xn, wlo_ref[...], preferred_element_type=F32)
              + jnp.dot(lo, whi, preferred_element_type=F32))
        t = fl + b_ref[...]
        logf = jnp.minimum(t, 0.0) - jnp.log(1.0 + jnp.exp(-jnp.abs(t)))
        o_refs[-1][...] = logf[:, :nh_logf]


def _proj_call(x2, g, w_bf, group, width, scales, dtypes, tm, logf_args=None):
    m, d = x2.shape
    ns = len(scales)
    const = lambda i: (0, 0)
    row = lambda i: (i, 0)
    in_specs = [pl.BlockSpec((tm, d), row),
                pl.BlockSpec((1, d), const),
                pl.BlockSpec((d, ns * width), lambda i: (0, group), pipeline_mode=pl.Buffered(1))]
    args = [x2, g, w_bf]
    out_specs = [pl.BlockSpec((tm, width), row)] * ns
    out_shape = [jax.ShapeDtypeStruct((m, width), dt) for dt in dtypes]
    nh = 0
    if logf_args is not None:
        wf_hi, wf_lo, b_pad, nh = logf_args
        in_specs += [pl.BlockSpec((d, LANES), const), pl.BlockSpec((d, LANES), const),
                     pl.BlockSpec((1, LANES), const)]
        args += [wf_hi, wf_lo, b_pad]
        out_specs.append(pl.BlockSpec((tm, nh), row))
        out_shape.append(jax.ShapeDtypeStruct((m, nh), F32))
    return pl.pallas_call(
        functools.partial(_proj_kernel, scales=scales, width=width, nh_logf=nh),
        grid=(m // tm,),
        in_specs=in_specs,
        out_specs=out_specs,
        out_shape=out_shape,
        compiler_params=_cparams(1),
        name=f"proj_group{group}",
    )(*args)


def _cumsum_kernel(x_ref, l_ref, o_ref, *, ch):
    ltri = l_ref[...]
    carry = jnp.zeros((1, x_ref.shape[1]), F32)
    for c in range(x_ref.shape[0] // ch):
        x1, x2, x3 = _split3(x_ref[c * ch:(c + 1) * ch, :])
        y = (jnp.dot(ltri, x1, preferred_element_type=F32)
             + jnp.dot(ltri, x2, preferred_element_type=F32)
             + jnp.dot(ltri, x3, preferred_element_type=F32)) + carry
        o_ref[c * ch:(c + 1) * ch, :] = y
        carry = y[ch - 1:ch, :]


def _cumsum_call(x, seq, ch):
    n, nh = x.shape
    tri = (jnp.arange(ch)[:, None] >= jnp.arange(ch)[None, :]).astype(BF16)
    return pl.pallas_call(
        functools.partial(_cumsum_kernel, ch=ch),
        grid=(n // seq,),
        in_specs=[pl.BlockSpec((seq, nh), lambda i: (i, 0)),
                  pl.BlockSpec((ch, ch), lambda i: (0, 0))],
        out_specs=pl.BlockSpec((seq, nh), lambda i: (i, 0)),
        out_shape=jax.ShapeDtypeStruct((n, nh), F32),
        compiler_params=_cparams(1),
        name="cumsum",
    )(x, tri)


def _lambda(lamv_ref, lam_init):
    lv = lamv_ref[...]
    return (jnp.exp(jnp.sum(lv[0:1] * lv[1:2], axis=1, keepdims=True))
            - jnp.exp(jnp.sum(lv[2:3] * lv[3:4], axis=1, keepdims=True)) + lam_init)


def _gate(g):
    g = g.astype(F32)
    return g / (1.0 + jnp.exp(-g))


def _stack_maps(q):
    lane = lax.broadcasted_iota(jnp.int32, q.shape, 1)
    zero = jnp.zeros_like(q)
    return jnp.concatenate([jnp.where(lane < HEAD // 2, q, zero),
                            jnp.where(lane >= HEAD // 2, q, zero)], axis=0)


def _self_attn_kernel(*refs, mode, tq, lam_init):
    it = iter(refs)
    q_ref, k_ref, v_ref, g_ref, tile_ref = next(it), next(it), next(it), next(it), next(it)
    if mode == "diff":
        slope_ref, lamv_ref, subg_ref = next(it), next(it), next(it)
    else:
        c_ref = next(it)
    o_ref = next(it)
    (kaug_ref, vt_ref, qaug_ref, m_ref, l_ref, acc_ref, sa_ref, sb_ref) = (
        next(it) for _ in range(8))

    h = pl.program_id(1)
    pi = pl.program_id(2)
    nq = vt_ref.shape[0]
    tk = tq
    n = qaug_ref.shape[1]
    blocks = (pi, nq - 1 - pi)
    if mode == "diff":
        bscale = slope_ref[h] * LOG2E

    @pl.when(pi == 0)
    def _():
        lane = lax.broadcasted_iota(jnp.int32, (tk, HEAD), 1)
        for j in range(nq):
            rows = slice(j * tk, (j + 1) * tk)
            if mode == "diff":
                pos = lax.broadcasted_iota(jnp.int32, (tk, 1), 0) + j * tk
                bias = bscale * pos.astype(F32)
            else:
                c8 = c_ref[0, rows, :]
                sel = lax.broadcasted_iota(jnp.int32, c8.shape, 1) == h
                bias = -LOG2E * jnp.sum(jnp.where(sel, c8, 0.0), axis=1, keepdims=True)
            b1, b2, b3 = (piece.astype(F32) for piece in _split3(bias))
            extra = jnp.where(lane == 0, b1, jnp.where(lane == 1, b2,
                                                       jnp.where(lane == 2, b3, 0.0)))
            kaug_ref[rows, :HEAD] = k_ref[0, rows, :].astype(BF16)
            kaug_ref[rows, HEAD:] = extra.astype(BF16)
            vt_ref[j] = v_ref[0, rows, :].T.astype(BF16)

    qlane = lax.broadcasted_iota(jnp.int32, (n, HEAD), 1)
    ones = jnp.where(qlane < N_BIAS, 1.0, 0.0).astype(BF16)
    for side, blk in enumerate(blocks):
        q = q_ref[0, pl.ds(pl.multiple_of(blk * tq, tq), tq), :]
        qs = _stack_maps(q) if mode == "diff" else q
        qaug_ref[side] = jnp.concatenate([qs, ones], axis=1)
    m_ref[...] = jnp.full(m_ref.shape, NEG, F32)
    l_ref[...] = jnp.zeros(l_ref.shape, F32)
    acc_ref[...] = jnp.zeros(acc_ref.shape, F32)

    groups = [slice(c0, c0 + QCOLS) for c0 in range(0, n, QCOLS)]

    def product(s):
        if isinstance(s, int) and s == 0:
            return 0, blocks[0]
        if isinstance(s, int) and s == nq:
            return 1, blocks[1]
        first = s <= blocks[0]
        past = jnp.where(first, s - 1, s - 1 - blocks[0])
        return jnp.where(first, 0, 1), jnp.where(s == 0, blocks[0], past)

    def scores(s, cols):
        side, j = product(s)
        k0 = pl.multiple_of(j * tk, tk)
        return lax.dot_general(kaug_ref[pl.ds(k0, tk), :], qaug_ref[side, cols, :], _NT,
                               preferred_element_type=F32)

    def consume(s_ref, s, cols):
        side, j = product(s)
        sc = s_ref[:, cols]
        m_prev = m_ref[side, :, cols]
        m_new = jnp.maximum(m_prev, jnp.max(sc, axis=0, keepdims=True))
        alpha = jnp.exp2(m_prev - m_new)
        p = jnp.exp2(sc - m_new)
        l_ref[side, :, cols] = alpha * l_ref[side, :, cols] + jnp.sum(p, axis=0, keepdims=True)
        acc_ref[side, :, cols] = alpha * acc_ref[side, :, cols] + jnp.dot(
            vt_ref[j], p.astype(BF16), preferred_element_type=F32)
        m_ref[side, :, cols] = m_new

    tile = tile_ref[...]
    if mode == "diff":
        tile = bscale * tile

    def pipe_step(s, cur_ref, nxt_ref, diagonal=False):
        for cols in groups:
            sc = scores(s, cols)
            if diagonal:
                t0 = cols.start % tq
                sc = sc + tile[:, t0:t0 + QCOLS]
            nxt_ref[:, cols] = sc
            if cur_ref is not None:
                consume(cur_ref, s - 1, cols)

    pipe_step(0, None, sa_ref, diagonal=True)

    def pipe_pair(pair, carry):
        pipe_step(2 * pair + 1, sa_ref, sb_ref)
        pipe_step(2 * pair + 2, sb_ref, sa_ref)
        return carry

    lax.fori_loop(0, (nq - 2) // 2, pipe_pair, 0)
    pipe_step(nq - 1, sa_ref, sb_ref)
    pipe_step(nq, sb_ref, sa_ref, diagonal=True)
    for cols in groups:
        consume(sa_ref, nq, cols)

    for side, blk in enumerate(blocks):
        rows = pl.ds(pl.multiple_of(blk * tq, tq), tq)
        ot = acc_ref[side] * (1.0 / l_ref[side])
        if mode == "diff":
            ot = ot[:, :tq] - _lambda(lamv_ref, lam_init) * ot[:, tq:]
        o = ot.T
        if mode == "diff":
            o = o * _rms_scale(o) * subg_ref[...] * (1.0 - lam_init)
        o_ref[0, rows, :] = (_gate(g_ref[0, rows, :]) * o).astype(o_ref.dtype)


def _self_attn_call(mode, q, k, v, g, tile, extra, *, tq, lam_init=0.0):
    b, t, w = q.shape
    nh = w // HEAD
    nq = t // tq
    assert nq % 2 == 0
    n = 2 * tq if mode == "diff" else tq
    seq_spec = pl.BlockSpec((1, t, HEAD), lambda bi, hi, pi: (bi, 0, hi))
    const2 = lambda bi, hi, pi: (0, 0)
    in_specs = [seq_spec, seq_spec, seq_spec, seq_spec, pl.BlockSpec((tq, tq), const2)]
    args = [q, k, v, g, tile]
    if mode == "diff":
        slopes, lamv, subg = extra
        in_specs += [pl.BlockSpec(memory_space=pltpu.SMEM),
                     pl.BlockSpec((4, LANES), const2),
                     pl.BlockSpec((1, HEAD), const2)]
        args += [slopes, lamv, subg]
    else:
        c = extra
        in_specs += [pl.BlockSpec((1, t, c.shape[2]), lambda bi, hi, pi: (bi, 0, 0))]
        args += [c]
    return pl.pallas_call(
        functools.partial(_self_attn_kernel, mode=mode, tq=tq, lam_init=lam_init),
        grid=(b, nh, nq // 2),
        in_specs=in_specs,
        out_specs=seq_spec,
        out_shape=jax.ShapeDtypeStruct((b, t, w), BF16),
        scratch_shapes=[pltpu.VMEM((t, 2 * HEAD), BF16), pltpu.VMEM((nq, HEAD, tq), BF16),
                        pltpu.VMEM((2, n, 2 * HEAD), BF16),
                        pltpu.VMEM((2, 1, n), F32), pltpu.VMEM((2, 1, n), F32),
                        pltpu.VMEM((2, HEAD, n), F32),
                        pltpu.VMEM((tq, n), F32), pltpu.VMEM((tq, n), F32)],
        compiler_params=_cparams(3),
        name=f"attn_{mode}_self",
    )(*args)


def _cache_attn_kernel(*refs, mode, nh, lp, lam_init):
    it = iter(refs)
    q_ref, kc_ref, vc_ref, kn_ref, vn_ref, g_ref, tile_ref = (next(it) for _ in range(7))
    if mode == "diff":
        slope_ref, lamv_ref, subg_ref = next(it), next(it), next(it)
    else:
        cp_ref, cd_ref = next(it), next(it)
    o_ref = next(it)
    tq = q_ref.shape[1]
    tile = tile_ref[...]
    col = lax.broadcasted_iota(jnp.int32, (1, lp), 1).astype(F32)

    for h in range(nh):
        lanes = slice(h * HEAD, (h + 1) * HEAD)
        heads = pl.ds(h, lp, stride=nh)
        q = q_ref[0, :, lanes]
        kn = kn_ref[0, :, lanes].astype(BF16)
        vn = vn_ref[0, :, lanes].astype(BF16)
        vc = vc_ref[0, heads, :].astype(BF16)
        if mode == "diff":
            qs = _stack_maps(q)
            kc = kc_ref[0, :, lanes].astype(BF16)
            bscale = slope_ref[h] * LOG2E
            bias_c = bscale * (col - float(lp))
            bias_n = bscale * jnp.concatenate([tile, tile], axis=0)
        else:
            qs = q
            kc = kc_ref[0, heads, :].astype(BF16)
            cn = cd_ref[0, h:h + 1, :]
            cref = cn[:, 0:1]
            bias_c = LOG2E * (cref - cp_ref[0, h:h + 1, :])
            bias_n = tile + LOG2E * (cref - cn)
        s_c = lax.dot_general(qs, kc, _NT, preferred_element_type=F32) + bias_c
        s_n = lax.dot_general(qs, kn, _NT, preferred_element_type=F32) + bias_n
        m = jnp.maximum(jnp.max(s_c, axis=1, keepdims=True), jnp.max(s_n, axis=1, keepdims=True))
        p_c = jnp.exp2(s_c - m)
        p_n = jnp.exp2(s_n - m)
        l = jnp.sum(p_c, axis=1, keepdims=True) + jnp.sum(p_n, axis=1, keepdims=True)
        acc = (jnp.dot(p_c.astype(BF16), vc, preferred_element_type=F32)
               + jnp.dot(p_n.astype(BF16), vn, preferred_element_type=F32))
        o = acc / l
        if mode == "diff":
            o = o[:tq] - _lambda(lamv_ref, lam_init) * o[tq:]
            o = o * _rms_scale(o) * subg_ref[...] * (1.0 - lam_init)
        o_ref[0, :, lanes] = (_gate(g_ref[0, :, lanes]) * o).astype(o_ref.dtype)


def _cache_attn_call(mode, q, kc, vc, kn, vn, g, tile, extra, *, nh, lp, lam_init=0.0):
    b, tq, w = q.shape
    row3 = lambda bi: (bi, 0, 0)
    const2 = lambda bi: (0, 0)
    new_spec = pl.BlockSpec((1, tq, w), row3)
    in_specs = [new_spec,
                pl.BlockSpec((1,) + kc.shape[1:], row3),
                pl.BlockSpec((1,) + vc.shape[1:], row3),
                new_spec, new_spec, new_spec,
                pl.BlockSpec((tq, tq), const2)]
    args = [q, kc, vc, kn, vn, g, tile]
    if mode == "diff":
        slopes, lamv, subg = extra
        in_specs += [pl.BlockSpec(memory_space=pltpu.SMEM),
                     pl.BlockSpec((4, LANES), const2),
                     pl.BlockSpec((1, HEAD), const2)]
        args += [slopes, lamv, subg]
    else:
        cp, cd = extra
        in_specs += [pl.BlockSpec((1,) + cp.shape[1:], row3),
                     pl.BlockSpec((1,) + cd.shape[1:], row3)]
        args += [cp, cd]
    return pl.pallas_call(
        functools.partial(_cache_attn_kernel, mode=mode, nh=nh, lp=lp, lam_init=lam_init),
        grid=(b,),
        in_specs=in_specs,
        out_specs=new_spec,
        out_shape=jax.ShapeDtypeStruct((b, tq, w), BF16),
        compiler_params=_cparams(1),
        name=f"attn_{mode}_cache",
    )(*args)


def _out_kernel(x_ref, ma_ref, mb_ref, wa_ref, wb_ref, gf_ref, o_ref):
    hres = (x_ref[...]
            + jnp.dot(ma_ref[...], wa_ref[...], preferred_element_type=F32)
            + jnp.dot(mb_ref[...], wb_ref[...], preferred_element_type=F32))
    o_ref[...] = hres * _rms_scale(hres) * gf_ref[...]


def _out_call(x2, ma, mb, wa, wb, gf, tm):
    m, d = x2.shape
    da, db = ma.shape[1], mb.shape[1]
    return pl.pallas_call(
        _out_kernel,
        grid=(m // tm,),
        in_specs=[pl.BlockSpec((tm, d), lambda i: (i, 0)),
                  pl.BlockSpec((tm, da), lambda i: (i, 0)),
                  pl.BlockSpec((tm, db), lambda i: (i, 0)),
                  pl.BlockSpec((da, d), lambda i: (0, 0)),
                  pl.BlockSpec((db, d), lambda i: (0, 0)),
                  pl.BlockSpec((1, d), lambda i: (0, 0))],
        out_specs=pl.BlockSpec((tm, d), lambda i: (i, 0)),
        out_shape=jax.ShapeDtypeStruct((m, d), F32),
        compiler_params=_cparams(1),
        name="out_proj",
    )(x2, ma, mb, wa, wb, gf)


def _chunk_allowed(q, k):
    return (k // CHUNK) <= (q // CHUNK)


def _diff_tile_kq(t):
    k = jnp.arange(t)[:, None]
    q = jnp.arange(t)[None, :]
    return jnp.where(_chunk_allowed(q, k), jnp.minimum(0, 2 * (q - k)).astype(F32), NEG)


def _causal_tile_kq(t):
    k = jnp.arange(t)[:, None]
    q = jnp.arange(t)[None, :]
    return jnp.where(k <= q, 0.0, NEG).astype(F32)


def _diff_tile_qk(t):
    q = jnp.arange(t)[:, None]
    k = jnp.arange(t)[None, :]
    return jnp.where(_chunk_allowed(q, k), (q - jnp.abs(q - k)).astype(F32), NEG)


def _causal_tile_qk(t):
    return _causal_tile_kq(t).T


def _layer(x, past, params, *, tm, tq, cum_ch):
    (g_norm, w_bf, wf_hi, wf_lo, b_pad, slopes, lamv, subg, lam_init,
     wo_a, wo_b, g_final, d_a, nh_b) = params
    b, s, d = x.shape
    nh_a = d_a // HEAD
    x2 = x.reshape(b * s, d)
    q_scale_a = float((HEAD // 2) ** -0.5 * LOG2E)
    q_scale_b = float(HEAD ** -0.5 * LOG2E)

    dtypes = (BF16, F32, F32, BF16)
    qa, ka, va, ga = _proj_call(x2, g_norm, w_bf, 0, d_a, (q_scale_a, 1.0, 1.0, 1.0), dtypes, tm)
    qb, kb, vb, gb, logf = _proj_call(x2, g_norm, w_bf, 1, d_a, (q_scale_b, 1.0, 1.0, 1.0),
                                      dtypes, tm, logf_args=(wf_hi, wf_lo, b_pad, nh_b))

    r3 = lambda a: a.reshape(b, s, a.shape[-1])
    if past is None:
        c = _cumsum_call(logf, s, cum_ch).reshape(b, s, nh_b)
        ma = _self_attn_call("diff", r3(qa), r3(ka), r3(va), r3(ga), _diff_tile_kq(tq),
                             (slopes, lamv, subg), tq=tq, lam_init=lam_init)
        mb = _self_attn_call("fox", r3(qb), r3(kb), r3(vb), r3(gb), _causal_tile_kq(tq), c, tq=tq)
    else:
        cak, cav, cbk, cbv, cbf = past
        lp = cak.shape[1]
        assert lp % CHUNK == 0 and s == tq <= CHUNK
        tot = lp + s
        padded = -(-tot // cum_ch) * cum_ch
        lf_all = jnp.concatenate([cbf, r3(logf), jnp.zeros((b, padded - tot, nh_b), F32)], axis=1)
        c = _cumsum_call(lf_all.reshape(b * padded, nh_b), padded, cum_ch)
        c = jnp.transpose(c.reshape(b, padded, nh_b), (0, 2, 1))
        slab = lambda a: a.reshape(b, lp * a.shape[2], HEAD)
        ma = _cache_attn_call("diff", r3(qa), cak.reshape(b, lp, d_a), slab(cav), r3(ka), r3(va),
                              r3(ga), _diff_tile_qk(tq), (slopes, lamv, subg),
                              nh=nh_a, lp=lp, lam_init=lam_init)
        mb = _cache_attn_call("fox", r3(qb), slab(cbk), slab(cbv), r3(kb), r3(vb), r3(gb),
                              _causal_tile_qk(tq), (c[:, :, :lp], c[:, :, lp:tot]),
                              nh=nh_b, lp=lp)

    y = _out_call(x2, ma.reshape(b * s, -1), mb.reshape(b * s, -1), wo_a, wo_b, g_final,
                  min(tm, 512))
    return y.reshape(b, s, d), ka, va, kb, vb, logf


def kernel(x_prompt, x_sample, cache_a_k, cache_a_v, cache_b_k, cache_b_v, cache_b_logf,
           g_norm, w_in, b_f, lam_q1, lam_k1, lam_q2, lam_k2, sub_gain, w_out, g_final):
    depth, d = g_norm.shape
    assert depth == 1
    nh_a2, hqk = cache_a_k.shape[3], cache_a_k.shape[4]
    nh_a = nh_a2 // 2
    nh_b = cache_b_k.shape[3]
    d_a = nh_a2 * hqk
    assert 2 * hqk == HEAD and cache_b_k.shape[4] == HEAD and d == 2 * d_a
    lyr = 0
    lam_init = 0.8 - 0.6 * math.exp(-0.3 * lyr)

    w = w_in[lyr]
    w_bf = w.astype(BF16)
    wf = jnp.pad(w[:, 8 * d_a:], ((0, 0), (0, LANES - nh_b)))
    wf_hi = wf.astype(BF16)
    wf_lo = (wf - wf_hi.astype(F32)).astype(BF16)
    b_pad = jnp.pad(b_f[lyr].astype(F32), (0, LANES - nh_b)).reshape(1, LANES)
    slopes = 2.0 ** (-8.0 * jnp.arange(1, nh_a + 1, dtype=F32) / nh_a)
    lamv = jnp.pad(jnp.stack([lam_q1[lyr], lam_k1[lyr], lam_q2[lyr], lam_k2[lyr]]).astype(F32),
                   ((0, 0), (0, LANES - hqk)))
    subg = sub_gain[lyr].astype(F32).reshape(1, HEAD)
    wo = w_out[lyr].astype(BF16)
    params = (g_norm[lyr].reshape(1, d), w_bf, wf_hi, wf_lo, b_pad, slopes, lamv, subg, lam_init,
              wo[:d_a], wo[d_a:], g_final.reshape(1, d), d_a, nh_b)

    bp, sp, _ = x_prompt.shape
    bs, ss, _ = x_sample.shape
    yp, pak, pav, pbk, pbv, pbf = _layer(x_prompt, None, params, tm=512, tq=512, cum_ch=256)
    past = (cache_a_k[lyr], cache_a_v[lyr], cache_b_k[lyr], cache_b_v[lyr], cache_b_logf[lyr])
    ys, sak, sav, sbk, sbv, sbf = _layer(x_sample, past, params, tm=bs * ss, tq=ss, cum_ch=256)

    shp = lambda a, bb, s_, n, e: a.reshape(1, bb, s_, n, e)
    return (yp, ys,
            shp(pak, bp, sp, nh_a2, hqk), shp(pav, bp, sp, nh_a, HEAD),
            shp(pbk, bp, sp, nh_b, HEAD), shp(pbv, bp, sp, nh_b, HEAD),
            pbf.reshape(1, bp, sp, nh_b),
            shp(sak, bs, ss, nh_a2, hqk), shp(sav, bs, ss, nh_a, HEAD),
            shp(sbk, bs, ss, nh_b, HEAD), shp(sbv, bs, ss, nh_b, HEAD),
            sbf.reshape(1, bs, ss, nh_b))
```

```python
import functools
import math

import jax
import jax.numpy as jnp
from jax import lax
from jax.experimental import pallas as pl
from jax.experimental.pallas import tpu as pltpu

F32 = jnp.float32
BF16 = jnp.bfloat16

EPS = 1e-6
CHUNK = 64
HEAD = 128
NEG = -1e30
LANES = 128
LOG2E = math.log2(math.e)
N_BIAS = 3
QCOLS = 256
VMEM_LIMIT = 56 * 1024 * 1024

_NT = (((1,), (1,)), ((), ()))


def _cparams(n_axes):
    return pltpu.CompilerParams(
        dimension_semantics=("arbitrary",) * n_axes, vmem_limit_bytes=VMEM_LIMIT)


def _rms_scale(x):
    return lax.rsqrt(jnp.mean(x * x, axis=-1, keepdims=True) + EPS)


def _split3(x):
    x1 = x.astype(BF16)
    r1 = x - x1.astype(F32)
    x2 = r1.astype(BF16)
    x3 = (r1 - x2.astype(F32)).astype(BF16)
    return x1, x2, x3


def _proj_kernel(*refs, scales, width, nh_logf):
    x_ref, g_ref, w_ref = refs[:3]
    o_refs = refs[-len(scales) - (1 if nh_logf else 0):]
    x = x_ref[...]
    xn32 = x * _rms_scale(x) * g_ref[...]
    xn = xn32.astype(BF16)
    for j, (scale, o_ref) in enumerate(zip(scales, o_refs)):
        z = jnp.dot(xn, w_ref[:, j * width:(j + 1) * width], preferred_element_type=F32)
        if scale != 1.0:
            z = z * scale
        o_ref[...] = z.astype(o_ref.dtype)
    if nh_logf:
        whi_ref, wlo_ref, b_ref = refs[3:6]
        lo = (xn32 - xn.astype(F32)).astype(BF16)
        whi = whi_ref[...]
        fl = (jnp.dot(xn, whi, preferred_element_type=F32)
              + jnp.dot(xn, wlo_ref[...], preferred_element_type=F32)
              + jnp.dot(lo, whi, preferred_element_type=F32))
        t = fl + b_ref[...]
        logf = jnp.minimum(t, 0.0) - jnp.log(1.0 + jnp.exp(-jnp.abs(t)))
        o_refs[-1][...] = logf[:, :nh_logf]


def _proj_call(x2, g, w_bf, group, width, scales, dtypes, tm, logf_args=None):
    m, d = x2.shape
    ns = len(scales)
    const = lambda i: (0, 0)
    row = lambda i: (i, 0)
    in_specs = [pl.BlockSpec((tm, d), row),
                pl.BlockSpec((1, d), const),
                pl.BlockSpec((d, ns * width), lambda i: (0, group), pipeline_mode=pl.Buffered(1))]
    args = [x2, g, w_bf]
    out_specs = [pl.BlockSpec((tm, width), row)] * ns
    out_shape = [jax.ShapeDtypeStruct((m, width), dt) for dt in dtypes]
    nh = 0
    if logf_args is not None:
        wf_hi, wf_lo, b_pad, nh = logf_args
        in_specs += [pl.BlockSpec((d, LANES), const), pl.BlockSpec((d, LANES), const),
                     pl.BlockSpec((1, LANES), const)]
        args += [wf_hi, wf_lo, b_pad]
        out_specs.append(pl.BlockSpec((tm, nh), row))
        out_shape.append(jax.ShapeDtypeStruct((m, nh), F32))
    return pl.pallas_call(
        functools.partial(_proj_kernel, scales=scales, width=width, nh_logf=nh),
        grid=(m // tm,),
        in_specs=in_specs,
        out_specs=out_specs,
        out_shape=out_shape,
        compiler_params=_cparams(1),
        name=f"proj_group{group}",
    )(*args)


def _cumsum_kernel(x_ref, l_ref, o_ref, *, ch):
    ltri = l_ref[...]
    carry = jnp.zeros((1, x_ref.shape[1]), F32)
    for c in range(x_ref.shape[0] // ch):
        x1, x2, x3 = _split3(x_ref[c * ch:(c + 1) * ch, :])
        y = (jnp.dot(ltri, x1, preferred_element_type=F32)
             + jnp.dot(ltri, x2, preferred_element_type=F32)
             + jnp.dot(ltri, x3, preferred_element_type=F32)) + carry
        o_ref[c * ch:(c + 1) * ch, :] = y
        carry = y[ch - 1:ch, :]


def _cumsum_call(x, seq, ch):
    n, nh = x.shape
    tri = (jnp.arange(ch)[:, None] >= jnp.arange(ch)[None, :]).astype(BF16)
    return pl.pallas_call(
        functools.partial(_cumsum_kernel, ch=ch),
        grid=(n // seq,),
        in_specs=[pl.BlockSpec((seq, nh), lambda i: (i, 0)),
                  pl.BlockSpec((ch, ch), lambda i: (0, 0))],
        out_specs=pl.BlockSpec((seq, nh), lambda i: (i, 0)),
        out_shape=jax.ShapeDtypeStruct((n, nh), F32),
        compiler_params=_cparams(1),
        name="cumsum",
    )(x, tri)


def _lambda(lamv_ref, lam_init):
    lv = lamv_ref[...]
    return (jnp.exp(jnp.sum(lv[0:1] * lv[1:2], axis=1, keepdims=True))
            - jnp.exp(jnp.sum(lv[2:3] * lv[3:4], axis=1, keepdims=True)) + lam_init)


def _gate(g):
    g = g.astype(F32)
    return g / (1.0 + jnp.exp(-g))


def _stack_maps(q):
    lane = lax.broadcasted_iota(jnp.int32, q.shape, 1)
    zero = jnp.zeros_like(q)
    return jnp.concatenate([jnp.where(lane < HEAD // 2, q, zero),
                            jnp.where(lane >= HEAD // 2, q, zero)], axis=0)


def _self_attn_kernel(*refs, mode, tq, hp, lam_init):
    it = iter(refs)
    q_ref, k_ref, v_ref, g_ref, tile_ref = next(it), next(it), next(it), next(it), next(it)
    if mode == "diff":
        slope_ref, lamv_ref, subg_ref = next(it), next(it), next(it)
    else:
        c_ref = next(it)
    o_ref = next(it)
    (kaug_ref, vt_ref, qaug_ref, m_ref, l_ref, acc_ref, sa_ref, sb_ref) = (
        next(it) for _ in range(8))

    hg = pl.program_id(1)
    nq = vt_ref.shape[1]
    tk = tq
    n = qaug_ref.shape[1]
    nhead = n // hp
    npairs = nq // 2
    if mode == "diff":
        bscale = slope_ref[hg] * LOG2E

    lane = lax.broadcasted_iota(jnp.int32, (tk, HEAD), 1)
    qlane = lax.broadcasted_iota(jnp.int32, (nhead, HEAD), 1)
    ones = jnp.where(qlane < N_BIAS, 1.0, 0.0).astype(BF16)
    for hs in range(hp):
        hl = slice(hs * HEAD, (hs + 1) * HEAD)
        for j in range(nq):
            rows = slice(j * tk, (j + 1) * tk)
            if mode == "diff":
                pos = lax.broadcasted_iota(jnp.int32, (tk, 1), 0) + j * tk
                bias = bscale * pos.astype(F32)
            else:
                c8 = c_ref[0, rows, :]
                sel = lax.broadcasted_iota(jnp.int32, c8.shape, 1) == hg * hp + hs
                bias = -LOG2E * jnp.sum(jnp.where(sel, c8, 0.0), axis=1, keepdims=True)
            b1, b2, b3 = (piece.astype(F32) for piece in _split3(bias))
            extra = jnp.where(lane == 0, b1, jnp.where(lane == 1, b2,
                                                       jnp.where(lane == 2, b3, 0.0)))
            kaug_ref[hs, rows, :HEAD] = k_ref[0, rows, hl].astype(BF16)
            kaug_ref[hs, rows, HEAD:] = extra.astype(BF16)
            vt_ref[hs, j] = v_ref[0, rows, hl].T.astype(BF16)
            q = q_ref[0, rows, hl]
            qs = _stack_maps(q) if mode == "diff" else q
            qaug_ref[j, hs * nhead:(hs + 1) * nhead, :] = jnp.concatenate([qs, ones], axis=1)
    m_ref[...] = jnp.full(m_ref.shape, NEG, F32)
    l_ref[...] = jnp.zeros(l_ref.shape, F32)
    acc_ref[...] = jnp.zeros(acc_ref.shape, F32)

    groups = [(c0 // nhead, slice(c0, c0 + QCOLS)) for c0 in range(0, n, QCOLS)]

    def product(p, s):
        if isinstance(s, int) and s == 0:
            return p, p
        if isinstance(s, int) and s == nq:
            return nq - 1 - p, nq - 1 - p
        first = s <= p
        past = jnp.where(first, s - 1, s - 1 - p)
        return jnp.where(first, p, nq - 1 - p), jnp.where(s == 0, p, past)

    def scores(p, s, group):
        hs, cols = group
        qb, j = product(p, s)
        k0 = j * tk if isinstance(j, int) else pl.multiple_of(j * tk, tk)
        return lax.dot_general(kaug_ref[hs, pl.ds(k0, tk), :], qaug_ref[qb, cols, :], _NT,
                               preferred_element_type=F32)

    def consume(s_ref, p, s, group):
        hs, cols = group
        qb, j = product(p, s)
        sc = s_ref[:, cols]
        m_prev = m_ref[qb, :, cols]
        m_new = jnp.maximum(m_prev, jnp.max(sc, axis=0, keepdims=True))
        alpha = jnp.exp2(m_prev - m_new)
        pr = jnp.exp2(sc - m_new)
        l_ref[qb, :, cols] = alpha * l_ref[qb, :, cols] + jnp.sum(pr, axis=0, keepdims=True)
        acc_ref[qb, :, cols] = alpha * acc_ref[qb, :, cols] + jnp.dot(
            vt_ref[hs, j], pr.astype(BF16), preferred_element_type=F32)
        m_ref[qb, :, cols] = m_new

    tile = tile_ref[...]
    if mode == "diff":
        tile = bscale * tile

    def pipe_step(nxt, cur, nxt_ref, cur_ref):
        for group in groups:
            sc = scores(*nxt, group)
            if isinstance(nxt[1], int) and nxt[1] in (0, nq):
                t0 = (group[1].start % nhead) % tq
                sc = sc + tile[:, t0:t0 + QCOLS]
            nxt_ref[:, group[1]] = sc
            if cur is not None:
                consume(cur_ref, *cur, group)

    def run_pair(p, a_ref, b_ref):
        def two_slots(i, carry):
            pipe_step((p, 2 * i + 1), (p, 2 * i), b_ref, a_ref)
            pipe_step((p, 2 * i + 2), (p, 2 * i + 1), a_ref, b_ref)
            return carry

        lax.fori_loop(0, (nq - 2) // 2, two_slots, 0)
        pipe_step((p, nq - 1), (p, nq - 2), b_ref, a_ref)
        pipe_step((p, nq), (p, nq - 1), a_ref, b_ref)
        pipe_step((jnp.minimum(p + 1, npairs - 1), 0), (p, nq), b_ref, a_ref)

    pipe_step((0, 0), None, sa_ref, None)

    def two_pairs(i, carry):
        run_pair(2 * i, sa_ref, sb_ref)
        run_pair(2 * i + 1, sb_ref, sa_ref)
        return carry

    lax.fori_loop(0, npairs // 2, two_pairs, 0)

    def finalize_block(blk):
        rows = pl.ds(pl.multiple_of(blk * tq, tq), tq)
        for hs in range(hp):
            hl = slice(hs * HEAD, (hs + 1) * HEAD)
            hc = slice(hs * nhead, (hs + 1) * nhead)
            ot = acc_ref[blk, :, hc] * (1.0 / l_ref[blk, :, hc])
            if mode == "diff":
                ot = ot[:, :tq] - _lambda(lamv_ref, lam_init) * ot[:, tq:]
            o = ot.T
            if mode == "diff":
                o = o * _rms_scale(o) * subg_ref[...] * (1.0 - lam_init)
            o_ref[0, rows, hl] = (_gate(g_ref[0, rows, hl]) * o).astype(o_ref.dtype)

    def finalize_two(i, carry):
        finalize_block(2 * i)
        finalize_block(2 * i + 1)
        return carry

    lax.fori_loop(0, nq // 2, finalize_two, 0)


def _self_attn_call(mode, q, k, v, g, tile, extra, *, tq, hp, lam_init=0.0):
    b, t, w = q.shape
    nh = w // HEAD
    nq = t // tq
    assert nq % 4 == 0 and nh % hp == 0 and (mode != "diff" or hp == 1)
    n = hp * (2 * tq if mode == "diff" else tq)
    seq_map = lambda bi, hi: (bi, 0, hi)
    seq_spec = pl.BlockSpec((1, t, hp * HEAD), seq_map)
    once_spec = pl.BlockSpec((1, t, hp * HEAD), seq_map, pipeline_mode=pl.Buffered(1))
    const2 = lambda bi, hi: (0, 0)
    in_specs = [seq_spec, once_spec, once_spec, seq_spec, pl.BlockSpec((tq, tq), const2)]
    args = [q, k, v, g, tile]
    if mode == "diff":
        slopes, lamv, subg = extra
        in_specs += [pl.BlockSpec(memory_space=pltpu.SMEM),
                     pl.BlockSpec((4, LANES), const2),
                     pl.BlockSpec((1, HEAD), const2)]
        args += [slopes, lamv, subg]
    else:
        c = extra
        in_specs += [pl.BlockSpec((1, t, c.shape[2]), lambda bi, hi: (bi, 0, 0))]
        args += [c]
    return pl.pallas_call(
        functools.partial(_self_attn_kernel, mode=mode, tq=tq, hp=hp, lam_init=lam_init),
        grid=(b, nh // hp),
        in_specs=in_specs,
        out_specs=seq_spec,
        out_shape=jax.ShapeDtypeStruct((b, t, w), BF16),
        scratch_shapes=[pltpu.VMEM((hp, t, 2 * HEAD), BF16),
                        pltpu.VMEM((hp, nq, HEAD, tq), BF16),
                        pltpu.VMEM((nq, n, 2 * HEAD), BF16),
                        pltpu.VMEM((nq, 1, n), F32), pltpu.VMEM((nq, 1, n), F32),
                        pltpu.VMEM((nq, HEAD, n), F32),
                        pltpu.VMEM((tq, n), F32), pltpu.VMEM((tq, n), F32)],
        compiler_params=_cparams(2),
        name=f"attn_{mode}_self",
    )(*args)


def _cache_attn_kernel(*refs, mode, nh, lp, lam_init):
    it = iter(refs)
    q_ref, kc_ref, vc_ref, kn_ref, vn_ref, g_ref, tile_ref = (next(it) for _ in range(7))
    if mode == "diff":
        slope_ref, lamv_ref, subg_ref = next(it), next(it), next(it)
    else:
        cp_ref, cd_ref = next(it), next(it)
    o_ref = next(it)
    tq = q_ref.shape[1]
    tile = tile_ref[...]
    col = lax.broadcasted_iota(jnp.int32, (1, lp), 1).astype(F32)

    for h in range(nh):
        lanes = slice(h * HEAD, (h + 1) * HEAD)
        heads = pl.ds(h, lp, stride=nh)
        q = q_ref[0, :, lanes]
        kn = kn_ref[0, :, lanes].astype(BF16)
        vn = vn_ref[0, :, lanes].astype(BF16)
        vc = vc_ref[0, heads, :].astype(BF16)
        if mode == "diff":
            qs = _stack_maps(q)
            kc = kc_ref[0, :, lanes].astype(BF16)
            bscale = slope_ref[h] * LOG2E
            bias_c = bscale * (col - float(lp))
            bias_n = bscale * jnp.concatenate([tile, tile], axis=0)
        else:
            qs = q
            kc = kc_ref[0, heads, :].astype(BF16)
            cn = cd_ref[0, h:h + 1, :]
            cref = cn[:, 0:1]
            bias_c = LOG2E * (cref - cp_ref[0, h:h + 1, :])
            bias_n = tile + LOG2E * (cref - cn)
        s_c = lax.dot_general(qs, kc, _NT, preferred_element_type=F32) + bias_c
        s_n = lax.dot_general(qs, kn, _NT, preferred_element_type=F32) + bias_n
        m = jnp.maximum(jnp.max(s_c, axis=1, keepdims=True), jnp.max(s_n, axis=1, keepdims=True))
        p_c = jnp.exp2(s_c - m)
        p_n = jnp.exp2(s_n - m)
        l = jnp.sum(p_c, axis=1, keepdims=True) + jnp.sum(p_n, axis=1, keepdims=True)
        acc = (jnp.dot(p_c.astype(BF16), vc, preferred_element_type=F32)
               + jnp.dot(p_n.astype(BF16), vn, preferred_element_type=F32))
        o = acc / l
        if mode == "diff":
            o = o[:tq] - _lambda(lamv_ref, lam_init) * o[tq:]
            o = o * _rms_scale(o) * subg_ref[...] * (1.0 - lam_init)
        o_ref[0, :, lanes] = (_gate(g_ref[0, :, lanes]) * o).astype(o_ref.dtype)


def _cache_attn_call(mode, q, kc, vc, kn, vn, g, tile, extra, *, nh, lp, lam_init=0.0):
    b, tq, w = q.shape
    row3 = lambda bi: (bi, 0, 0)
    const2 = lambda bi: (0, 0)
    new_spec = pl.BlockSpec((1, tq, w), row3)
    in_specs = [new_spec,
                pl.BlockSpec((1,) + kc.shape[1:], row3),
                pl.BlockSpec((1,) + vc.shape[1:], row3),
                new_spec, new_spec, new_spec,
                pl.BlockSpec((tq, tq), const2)]
    args = [q, kc, vc, kn, vn, g, tile]
    if mode == "diff":
        slopes, lamv, subg = extra
        in_specs += [pl.BlockSpec(memory_space=pltpu.SMEM),
                     pl.BlockSpec((4, LANES), const2),
                     pl.BlockSpec((1, HEAD), const2)]
        args += [slopes, lamv, subg]
    else:
        cp, cd = extra
        in_specs += [pl.BlockSpec((1,) + cp.shape[1:], row3),
                     pl.BlockSpec((1,) + cd.shape[1:], row3)]
        args += [cp, cd]
    return pl.pallas_call(
        functools.partial(_cache_attn_kernel, mode=mode, nh=nh, lp=lp, lam_init=lam_init),
        grid=(b,),
        in_specs=in_specs,
        out_specs=new_spec,
        out_shape=jax.ShapeDtypeStruct((b, tq, w), BF16),
        compiler_params=_cparams(1),
        name=f"attn_{mode}_cache",
    )(*args)


def _out_kernel(x_ref, ma_ref, mb_ref, wa_ref, wb_ref, gf_ref, o_ref):
    hres = (x_ref[...]
            + jnp.dot(ma_ref[...], wa_ref[...], preferred_element_type=F32)
            + jnp.dot(mb_ref[...], wb_ref[...], preferred_element_type=F32))
    o_ref[...] = hres * _rms_scale(hres) * gf_ref[...]


def _out_call(x2, ma, mb, wa, wb, gf, tm):
    m, d = x2.shape
    da, db = ma.shape[1], mb.shape[1]
    return pl.pallas_call(
        _out_kernel,
        grid=(m // tm,),
        in_specs=[pl.BlockSpec((tm, d), lambda i: (i, 0)),
                  pl.BlockSpec((tm, da), lambda i: (i, 0)),
                  pl.BlockSpec((tm, db), lambda i: (i, 0)),
                  pl.BlockSpec((da, d), lambda i: (0, 0)),
                  pl.BlockSpec((db, d), lambda i: (0, 0)),
                  pl.BlockSpec((1, d), lambda i: (0, 0))],
        out_specs=pl.BlockSpec((tm, d), lambda i: (i, 0)),
        out_shape=jax.ShapeDtypeStruct((m, d), F32),
        compiler_params=_cparams(1),
        name="out_proj",
    )(x2, ma, mb, wa, wb, gf)


def _chunk_allowed(q, k):
    return (k // CHUNK) <= (q // CHUNK)


def _diff_tile_kq(t):
    k = jnp.arange(t)[:, None]
    q = jnp.arange(t)[None, :]
    return jnp.where(_chunk_allowed(q, k), jnp.minimum(0, 2 * (q - k)).astype(F32), NEG)


def _causal_tile_kq(t):
    k = jnp.arange(t)[:, None]
    q = jnp.arange(t)[None, :]
    return jnp.where(k <= q, 0.0, NEG).astype(F32)


def _diff_tile_qk(t):
    q = jnp.arange(t)[:, None]
    k = jnp.arange(t)[None, :]
    return jnp.where(_chunk_allowed(q, k), (q - jnp.abs(q - k)).astype(F32), NEG)


def _causal_tile_qk(t):
    return _causal_tile_kq(t).T


def _layer(x, past, params, *, tm, tq, cum_ch):
    (g_norm, w_bf, wf_hi, wf_lo, b_pad, slopes, lamv, subg, lam_init,
     wo_a, wo_b, g_final, d_a, nh_b) = params
    b, s, d = x.shape
    nh_a = d_a // HEAD
    x2 = x.reshape(b * s, d)
    q_scale_a = float((HEAD // 2) ** -0.5 * LOG2E)
    q_scale_b = float(HEAD ** -0.5 * LOG2E)

    dtypes = (BF16, F32, F32, BF16)
    qa, ka, va, ga = _proj_call(x2, g_norm, w_bf, 0, d_a, (q_scale_a, 1.0, 1.0, 1.0), dtypes, tm)
    qb, kb, vb, gb, logf = _proj_call(x2, g_norm, w_bf, 1, d_a, (q_scale_b, 1.0, 1.0, 1.0),
                                      dtypes, tm, logf_args=(wf_hi, wf_lo, b_pad, nh_b))

    r3 = lambda a: a.reshape(b, s, a.shape[-1])
    if past is None:
        c = _cumsum_call(logf, s, cum_ch).reshape(b, s, nh_b)
        ma = _self_attn_call("diff", r3(qa), r3(ka), r3(va), r3(ga), _diff_tile_kq(tq),
                             (slopes, lamv, subg), tq=tq, hp=1, lam_init=lam_init)
        mb = _self_attn_call("fox", r3(qb), r3(kb), r3(vb), r3(gb), _causal_tile_kq(tq), c,
                             tq=tq, hp=2)
    else:
        cak, cav, cbk, cbv, cbf = past
        lp = cak.shape[1]
        assert lp % CHUNK == 0 and s == tq <= CHUNK
        tot = lp + s
        padded = -(-tot // cum_ch) * cum_ch
        lf_all = jnp.concatenate([cbf, r3(logf), jnp.zeros((b, padded - tot, nh_b), F32)], axis=1)
        c = _cumsum_call(lf_all.reshape(b * padded, nh_b), padded, cum_ch)
        c = jnp.transpose(c.reshape(b, padded, nh_b), (0, 2, 1))
        slab = lambda a: a.reshape(b, lp * a.shape[2], HEAD)
        ma = _cache_attn_call("diff", r3(qa), cak.reshape(b, lp, d_a), slab(cav), r3(ka), r3(va),
                              r3(ga), _diff_tile_qk(tq), (slopes, lamv, subg),
                              nh=nh_a, lp=lp, lam_init=lam_init)
        mb = _cache_attn_call("fox", r3(qb), slab(cbk), slab(cbv), r3(kb), r3(vb), r3(gb),
                              _causal_tile_qk(tq), (c[:, :, :lp], c[:, :, lp:tot]),
                              nh=nh_b, lp=lp)

    y = _out_call(x2, ma.reshape(b * s, -1), mb.reshape(b * s, -1), wo_a, wo_b, g_final,
                  min(tm, 512))
    return y.reshape(b, s, d), ka, va, kb, vb, logf


def kernel(x_prompt, x_sample, cache_a_k, cache_a_v, cache_b_k, cache_b_v, cache_b_logf,
           g_norm, w_in, b_f, lam_q1, lam_k1, lam_q2, lam_k2, sub_gain, w_out, g_final):
    depth, d = g_norm.shape
    assert depth == 1
    nh_a2, hqk = cache_a_k.shape[3], cache_a_k.shape[4]
    nh_a = nh_a2 // 2
    nh_b = cache_b_k.shape[3]
    d_a = nh_a2 * hqk
    assert 2 * hqk == HEAD and cache_b_k.shape[4] == HEAD and d == 2 * d_a
    lyr = 0
    lam_init = 0.8 - 0.6 * math.exp(-0.3 * lyr)

    w = w_in[lyr]
    w_bf = w.astype(BF16)
    wf = jnp.pad(w[:, 8 * d_a:], ((0, 0), (0, LANES - nh_b)))
    wf_hi = wf.astype(BF16)
    wf_lo = (wf - wf_hi.astype(F32)).astype(BF16)
    b_pad = jnp.pad(b_f[lyr].astype(F32), (0, LANES - nh_b)).reshape(1, LANES)
    slopes = 2.0 ** (-8.0 * jnp.arange(1, nh_a + 1, dtype=F32) / nh_a)
    lamv = jnp.pad(jnp.stack([lam_q1[lyr], lam_k1[lyr], lam_q2[lyr], lam_k2[lyr]]).astype(F32),
                   ((0, 0), (0, LANES - hqk)))
    subg = sub_gain[lyr].astype(F32).reshape(1, HEAD)
    wo = w_out[lyr].astype(BF16)
    params = (g_norm[lyr].reshape(1, d), w_bf, wf_hi, wf_lo, b_pad, slopes, lamv, subg, lam_init,
              wo[:d_a], wo[d_a:], g_final.reshape(1, d), d_a, nh_b)

    bp, sp, _ = x_prompt.shape
    bs, ss, _ = x_sample.shape
    yp, pak, pav, pbk, pbv, pbf = _layer(x_prompt, None, params, tm=512, tq=512, cum_ch=256)
    past = (cache_a_k[lyr], cache_a_v[lyr], cache_b_k[lyr], cache_b_v[lyr], cache_b_logf[lyr])
    ys, sak, sav, sbk, sbv, sbf = _layer(x_sample, past, params, tm=bs * ss, tq=ss, cum_ch=256)

    shp = lambda a, bb, s_, n, e: a.reshape(1, bb, s_, n, e)
    return (yp, ys,
            shp(pak, bp, sp, nh_a2, hqk), shp(pav, bp, sp, nh_a, HEAD),
            shp(pbk, bp, sp, nh_b, HEAD), shp(pbv, bp, sp, nh_b, HEAD),
            pbf.reshape(1, bp, sp, nh_b),
            shp(sak, bs, ss, nh_a2, hqk), shp(sav, bs, ss, nh_a, HEAD),
            shp(sbk, bs, ss, nh_b, HEAD), shp(sbv, bs, ss, nh_b, HEAD),
            sbf.reshape(1, bs, ss, nh_b))
```

```python
import functools
import math

import jax
import jax.numpy as jnp
from jax import lax
from jax.experimental import pallas as pl
from jax.experimental.pallas import tpu as pltpu

F32 = jnp.float32
BF16 = jnp.bfloat16

EPS = 1e-6
CHUNK = 64
HEAD = 128
NEG = -1e30
LANES = 128
LOG2E = math.log2(math.e)
N_BIAS = 3
QCOLS = 256
VMEM_LIMIT = 56 * 1024 * 1024

_NT = (((1,), (1,)), ((), ()))


def _cparams(n_axes):
    return pltpu.CompilerParams(
        dimension_semantics=("arbitrary",) * n_axes, vmem_limit_bytes=VMEM_LIMIT)


def _rms_scale(x):
    return lax.rsqrt(jnp.mean(x * x, axis=-1, keepdims=True) + EPS)


def _split3(x):
    x1 = x.astype(BF16)
    r1 = x - x1.astype(F32)
    x2 = r1.astype(BF16)
    x3 = (r1 - x2.astype(F32)).astype(BF16)
    return x1, x2, x3


def _proj_kernel(*refs, q_scale, width, k_minor, attn_copies, nh_logf):
    x_ref, g_ref, w_ref = refs[:3]
    n_out = 4 + (2 if attn_copies else 0) + (1 if nh_logf else 0)
    o_refs = refs[-n_out:]
    q_ref, k_ref, v_ref, gate_ref = o_refs[:4]
    x = x_ref[...]
    xn32 = x * _rms_scale(x) * g_ref[...]
    xn = xn32.astype(BF16)
    slab = lambda j: jnp.dot(xn, w_ref[:, j * width:(j + 1) * width],
                             preferred_element_type=F32)
    q_ref[...] = (slab(0) * q_scale).astype(q_ref.dtype)
    k = slab(1)
    k_ref[...] = k.reshape(k_ref.shape)
    v = slab(2)
    v_ref[...] = v
    gate_ref[...] = slab(3).astype(gate_ref.dtype)
    if attn_copies:
        kbf_ref, vt_ref = o_refs[4:6]
        kbf_ref[...] = k.astype(BF16)
        for h in range(width // HEAD):
            vt_ref[h, 0] = v[:, h * HEAD:(h + 1) * HEAD].T.astype(BF16)
    if nh_logf:
        whi_ref, wlo_ref, b_ref = refs[3:6]
        lo = (xn32 - xn.astype(F32)).astype(BF16)
        whi = whi_ref[...]
        fl = (jnp.dot(xn, whi, preferred_element_type=F32)
              + jnp.dot(xn, wlo_ref[...], preferred_element_type=F32)
              + jnp.dot(lo, whi, preferred_element_type=F32))
        t = fl + b_ref[...]
        logf = jnp.minimum(t, 0.0) - jnp.log(1.0 + jnp.exp(-jnp.abs(t)))
        o_refs[-1][...] = logf[:, :nh_logf]


def _proj_call(x2, g, w_bf, group, width, q_scale, k_minor, tm, attn_copies, logf_args=None):
    m, d = x2.shape
    const = lambda i: (0, 0)
    row = lambda i: (i, 0)
    krow = lambda i: (i,) + (0,) * len(k_minor)
    in_specs = [pl.BlockSpec((tm, d), row),
                pl.BlockSpec((1, d), const),
                pl.BlockSpec((d, 4 * width), lambda i: (0, group), pipeline_mode=pl.Buffered(1))]
    args = [x2, g, w_bf]
    flat = pl.BlockSpec((tm, width), row)
    out_specs = [flat, pl.BlockSpec((tm,) + k_minor, krow), flat, flat]
    out_shape = [jax.ShapeDtypeStruct((m, width), BF16),
                 jax.ShapeDtypeStruct((m,) + k_minor, F32),
                 jax.ShapeDtypeStruct((m, width), F32),
                 jax.ShapeDtypeStruct((m, width), BF16)]
    if attn_copies:
        nh = width // HEAD
        out_specs += [flat, pl.BlockSpec((nh, 1, HEAD, tm), lambda i: (0, i, 0, 0))]
        out_shape += [jax.ShapeDtypeStruct((m, width), BF16),
                      jax.ShapeDtypeStruct((nh, m // tm, HEAD, tm), BF16)]
    nh_logf = 0
    if logf_args is not None:
        wf_hi, wf_lo, b_pad, nh_logf = logf_args
        in_specs += [pl.BlockSpec((d, LANES), const), pl.BlockSpec((d, LANES), const),
                     pl.BlockSpec((1, LANES), const)]
        args += [wf_hi, wf_lo, b_pad]
        out_specs.append(pl.BlockSpec((tm, nh_logf), row))
        out_shape.append(jax.ShapeDtypeStruct((m, nh_logf), F32))
    return pl.pallas_call(
        functools.partial(_proj_kernel, q_scale=q_scale, width=width, k_minor=k_minor,
                          attn_copies=attn_copies, nh_logf=nh_logf),
        grid=(m // tm,),
        in_specs=in_specs,
        out_specs=out_specs,
        out_shape=out_shape,
        compiler_params=_cparams(1),
        name=f"proj_group{group}",
    )(*args)


def _cumsum_kernel(x_ref, l_ref, o_ref, *, ch):
    ltri = l_ref[...]
    carry = jnp.zeros((1, x_ref.shape[1]), F32)
    for c in range(x_ref.shape[0] // ch):
        x1, x2, x3 = _split3(x_ref[c * ch:(c + 1) * ch, :])
        y = (jnp.dot(ltri, x1, preferred_element_type=F32)
             + jnp.dot(ltri, x2, preferred_element_type=F32)
             + jnp.dot(ltri, x3, preferred_element_type=F32)) + carry
        o_ref[c * ch:(c + 1) * ch, :] = y
        carry = y[ch - 1:ch, :]


def _cumsum_call(x, seq, ch):
    n, nh = x.shape
    tri = (jnp.arange(ch)[:, None] >= jnp.arange(ch)[None, :]).astype(BF16)
    return pl.pallas_call(
        functools.partial(_cumsum_kernel, ch=ch),
        grid=(n // seq,),
        in_specs=[pl.BlockSpec((seq, nh), lambda i: (i, 0)),
                  pl.BlockSpec((ch, ch), lambda i: (0, 0))],
        out_specs=pl.BlockSpec((seq, nh), lambda i: (i, 0)),
        out_shape=jax.ShapeDtypeStruct((n, nh), F32),
        compiler_params=_cparams(1),
        name="cumsum",
    )(x, tri)


def _lambda(lamv_ref, lam_init):
    lv = lamv_ref[...]
    return (jnp.exp(jnp.sum(lv[0:1] * lv[1:2], axis=1, keepdims=True))
            - jnp.exp(jnp.sum(lv[2:3] * lv[3:4], axis=1, keepdims=True)) + lam_init)


def _gate(g):
    g = g.astype(F32)
    return g / (1.0 + jnp.exp(-g))


def _stack_maps(q):
    lane = lax.broadcasted_iota(jnp.int32, q.shape, 1)
    zero = jnp.zeros_like(q)
    return jnp.concatenate([jnp.where(lane < HEAD // 2, q, zero),
                            jnp.where(lane >= HEAD // 2, q, zero)], axis=0)


def _self_attn_kernel(*refs, mode, tq, hp, lam_init):
    it = iter(refs)
    q_ref, k_ref, vt_ref, g_ref, tile_ref = next(it), next(it), next(it), next(it), next(it)
    if mode == "diff":
        slope_ref, lamv_ref, subg_ref = next(it), next(it), next(it)
    else:
        c_ref = next(it)
    o_ref = next(it)
    kaug_ref, qaug_ref, m_ref, l_ref, acc_ref, sa_ref, sb_ref = (next(it) for _ in range(7))

    hg = pl.program_id(1)
    nq = vt_ref.shape[1]
    tk = tq
    n = qaug_ref.shape[1]
    nhead = n // hp
    npairs = nq // 2
    if mode == "diff":
        bscale = slope_ref[hg] * LOG2E

    lane = lax.broadcasted_iota(jnp.int32, (tk, HEAD), 1)
    qlane = lax.broadcasted_iota(jnp.int32, (nhead, HEAD), 1)
    ones = jnp.where(qlane < N_BIAS, 1.0, 0.0).astype(BF16)
    for hs in range(hp):
        hl = slice(hs * HEAD, (hs + 1) * HEAD)
        for j in range(nq):
            rows = slice(j * tk, (j + 1) * tk)
            if mode == "diff":
                pos = lax.broadcasted_iota(jnp.int32, (tk, 1), 0) + j * tk
                bias = bscale * pos.astype(F32)
            else:
                c8 = c_ref[0, rows, :]
                sel = lax.broadcasted_iota(jnp.int32, c8.shape, 1) == hg * hp + hs
                bias = -LOG2E * jnp.sum(jnp.where(sel, c8, 0.0), axis=1, keepdims=True)
            b1, b2, b3 = (piece.astype(F32) for piece in _split3(bias))
            extra = jnp.where(lane == 0, b1, jnp.where(lane == 1, b2,
                                                       jnp.where(lane == 2, b3, 0.0)))
            kaug_ref[hs, rows, :HEAD] = k_ref[0, rows, hl]
            kaug_ref[hs, rows, HEAD:] = extra.astype(BF16)
            q = q_ref[0, rows, hl]
            qs = _stack_maps(q) if mode == "diff" else q
            qaug_ref[j, hs * nhead:(hs + 1) * nhead, :] = jnp.concatenate([qs, ones], axis=1)
    m_ref[...] = jnp.full(m_ref.shape, NEG, F32)
    l_ref[...] = jnp.zeros(l_ref.shape, F32)
    acc_ref[...] = jnp.zeros(acc_ref.shape, F32)

    groups = [(c0 // nhead, slice(c0, c0 + QCOLS)) for c0 in range(0, n, QCOLS)]

    def product(p, s):
        if isinstance(s, int) and s == 0:
            return p, p
        if isinstance(s, int) and s == nq:
            return nq - 1 - p, nq - 1 - p
        first = s <= p
        past = jnp.where(first, s - 1, s - 1 - p)
        return jnp.where(first, p, nq - 1 - p), jnp.where(s == 0, p, past)

    def scores(p, s, group):
        hs, cols = group
        qb, j = product(p, s)
        k0 = j * tk if isinstance(j, int) else pl.multiple_of(j * tk, tk)
        return lax.dot_general(kaug_ref[hs, pl.ds(k0, tk), :], qaug_ref[qb, cols, :], _NT,
                               preferred_element_type=F32)

    def consume(s_ref, p, s, group):
        hs, cols = group
        qb, j = product(p, s)
        sc = s_ref[:, cols]
        m_prev = m_ref[qb, :, cols]
        m_new = jnp.maximum(m_prev, jnp.max(sc, axis=0, keepdims=True))
        alpha = jnp.exp2(m_prev - m_new)
        pr = jnp.exp2(sc - m_new)
        l_ref[qb, :, cols] = alpha * l_ref[qb, :, cols] + jnp.sum(pr, axis=0, keepdims=True)
        acc_ref[qb, :, cols] = alpha * acc_ref[qb, :, cols] + jnp.dot(
            vt_ref[hs, j], pr.astype(BF16), preferred_element_type=F32)
        m_ref[qb, :, cols] = m_new

    tile = tile_ref[...]
    if mode == "diff":
        tile = bscale * tile

    def pipe_step(nxt, cur, nxt_ref, cur_ref):
        for group in groups:
            sc = scores(*nxt, group)
            if isinstance(nxt[1], int) and nxt[1] in (0, nq):
                t0 = (group[1].start % nhead) % tq
                sc = sc + tile[:, t0:t0 + QCOLS]
            nxt_ref[:, group[1]] = sc
            if cur is not None:
                consume(cur_ref, *cur, group)

    def run_pair(p, a_ref, b_ref):
        def two_slots(i, carry):
            pipe_step((p, 2 * i + 1), (p, 2 * i), b_ref, a_ref)
            pipe_step((p, 2 * i + 2), (p, 2 * i + 1), a_ref, b_ref)
            return carry

        lax.fori_loop(0, (nq - 2) // 2, two_slots, 0)
        pipe_step((p, nq - 1), (p, nq - 2), b_ref, a_ref)
        pipe_step((p, nq), (p, nq - 1), a_ref, b_ref)
        pipe_step((jnp.minimum(p + 1, npairs - 1), 0), (p, nq), b_ref, a_ref)

    pipe_step((0, 0), None, sa_ref, None)

    def two_pairs(i, carry):
        run_pair(2 * i, sa_ref, sb_ref)
        run_pair(2 * i + 1, sb_ref, sa_ref)
        return carry

    lax.fori_loop(0, npairs // 2, two_pairs, 0)

    def finalize_block(blk):
        rows = pl.ds(pl.multiple_of(blk * tq, tq), tq)
        for hs in range(hp):
            hl = slice(hs * HEAD, (hs + 1) * HEAD)
            hc = slice(hs * nhead, (hs + 1) * nhead)
            ot = acc_ref[blk, :, hc] * (1.0 / l_ref[blk, :, hc])
            if mode == "diff":
                ot = ot[:, :tq] - _lambda(lamv_ref, lam_init) * ot[:, tq:]
            o = ot.T
            if mode == "diff":
                o = o * _rms_scale(o) * subg_ref[...] * (1.0 - lam_init)
            o_ref[0, rows, hl] = (_gate(g_ref[0, rows, hl]) * o).astype(o_ref.dtype)

    def finalize_two(i, carry):
        finalize_block(2 * i)
        finalize_block(2 * i + 1)
        return carry

    lax.fori_loop(0, nq // 2, finalize_two, 0)


def _self_attn_call(mode, q, k, vt, g, tile, extra, *, tq, hp, lam_init=0.0):
    b, t, w = q.shape
    nh = w // HEAD
    nq = t // tq
    assert nq % 4 == 0 and nh % hp == 0 and (mode != "diff" or hp == 1)
    assert vt.shape == (nh, b * nq, HEAD, tq)
    n = hp * (2 * tq if mode == "diff" else tq)
    seq_spec = pl.BlockSpec((1, t, hp * HEAD), lambda bi, hi: (bi, 0, hi))
    const2 = lambda bi, hi: (0, 0)
    in_specs = [seq_spec, seq_spec,
                pl.BlockSpec((hp, nq, HEAD, tq), lambda bi, hi: (hi, bi, 0, 0)),
                seq_spec,
                pl.BlockSpec((tq, tq), const2, pipeline_mode=pl.Buffered(1))]
    args = [q, k, vt, g, tile]
    if mode == "diff":
        slopes, lamv, subg = extra
        in_specs += [pl.BlockSpec(memory_space=pltpu.SMEM),
                     pl.BlockSpec((4, LANES), const2),
                     pl.BlockSpec((1, HEAD), const2)]
        args += [slopes, lamv, subg]
    else:
        c = extra
        in_specs += [pl.BlockSpec((1, t, c.shape[2]), lambda bi, hi: (bi, 0, 0),
                                  pipeline_mode=pl.Buffered(1))]
        args += [c]
    return pl.pallas_call(
        functools.partial(_self_attn_kernel, mode=mode, tq=tq, hp=hp, lam_init=lam_init),
        grid=(b, nh // hp),
        in_specs=in_specs,
        out_specs=seq_spec,
        out_shape=jax.ShapeDtypeStruct((b, t, w), BF16),
        scratch_shapes=[pltpu.VMEM((hp, t, 2 * HEAD), BF16),
                        pltpu.VMEM((nq, n, 2 * HEAD), BF16),
                        pltpu.VMEM((nq, 1, n), F32), pltpu.VMEM((nq, 1, n), F32),
                        pltpu.VMEM((nq, HEAD, n), F32),
                        pltpu.VMEM((tq, n), F32), pltpu.VMEM((tq, n), F32)],
        compiler_params=_cparams(2),
        name=f"attn_{mode}_self",
    )(*args)


def _cache_attn_kernel(*refs, mode, nh, lp, lam_init):
    it = iter(refs)
    q_ref, kc_ref, vc_ref, kn_ref, vn_ref, g_ref, tile_ref = (next(it) for _ in range(7))
    if mode == "diff":
        slope_ref, lamv_ref, subg_ref = next(it), next(it), next(it)
    else:
        cp_ref, cd_ref = next(it), next(it)
    o_ref = next(it)
    tq = q_ref.shape[1]
    tile = tile_ref[...]
    col = lax.broadcasted_iota(jnp.int32, (1, lp), 1).astype(F32)
    if mode == "diff":
        kc_all = kc_ref[0].reshape(lp, nh * HEAD).astype(BF16)
        kn_all = kn_ref[0].reshape(tq, nh * HEAD).astype(BF16)

    for h in range(nh):
        lanes = slice(h * HEAD, (h + 1) * HEAD)
        heads = pl.ds(h, lp, stride=nh)
        q = q_ref[0, :, lanes]
        vn = vn_ref[0, :, lanes].astype(BF16)
        vc = vc_ref[0, heads, :].astype(BF16)
        if mode == "diff":
            qs = _stack_maps(q)
            kc = kc_all[:, lanes]
            kn = kn_all[:, lanes]
            bscale = slope_ref[h] * LOG2E
            bias_c = bscale * (col - float(lp))
            bias_n = bscale * jnp.concatenate([tile, tile], axis=0)
        else:
            qs = q
            kc = kc_ref[0, heads, :].astype(BF16)
            kn = kn_ref[0, :, lanes].astype(BF16)
            cn = cd_ref[0, h:h + 1, :]
            cref = cn[:, 0:1]
            bias_c = LOG2E * (cref - cp_ref[0, h:h + 1, :])
            bias_n = tile + LOG2E * (cref - cn)
        s_c = lax.dot_general(qs, kc, _NT, preferred_element_type=F32) + bias_c
        s_n = lax.dot_general(qs, kn, _NT, preferred_element_type=F32) + bias_n
        m = jnp.maximum(jnp.max(s_c, axis=1, keepdims=True), jnp.max(s_n, axis=1, keepdims=True))
        p_c = jnp.exp2(s_c - m)
        p_n = jnp.exp2(s_n - m)
        l = jnp.sum(p_c, axis=1, keepdims=True) + jnp.sum(p_n, axis=1, keepdims=True)
        acc = (jnp.dot(p_c.astype(BF16), vc, preferred_element_type=F32)
               + jnp.dot(p_n.astype(BF16), vn, preferred_element_type=F32))
        o = acc / l
        if mode == "diff":
            o = o[:tq] - _lambda(lamv_ref, lam_init) * o[tq:]
            o = o * _rms_scale(o) * subg_ref[...] * (1.0 - lam_init)
        o_ref[0, :, lanes] = (_gate(g_ref[0, :, lanes]) * o).astype(o_ref.dtype)


def _cache_attn_call(mode, q, kc, vc, kn, vn, g, tile, extra, *, nh, lp, lam_init=0.0):
    b, tq, w = q.shape
    row3 = lambda bi: (bi, 0, 0)
    const2 = lambda bi: (0, 0)
    new_spec = pl.BlockSpec((1, tq, w), row3)
    whole = lambda a: pl.BlockSpec((1,) + a.shape[1:], lambda bi: (bi,) + (0,) * (a.ndim - 1))
    in_specs = [new_spec, whole(kc), whole(vc), whole(kn), new_spec, new_spec,
                pl.BlockSpec((tq, tq), const2)]
    args = [q, kc, vc, kn, vn, g, tile]
    if mode == "diff":
        slopes, lamv, subg = extra
        in_specs += [pl.BlockSpec(memory_space=pltpu.SMEM),
                     pl.BlockSpec((4, LANES), const2),
                     pl.BlockSpec((1, HEAD), const2)]
        args += [slopes, lamv, subg]
    else:
        cp, cd = extra
        in_specs += [pl.BlockSpec((1,) + cp.shape[1:], row3),
                     pl.BlockSpec((1,) + cd.shape[1:], row3)]
        args += [cp, cd]
    return pl.pallas_call(
        functools.partial(_cache_attn_kernel, mode=mode, nh=nh, lp=lp, lam_init=lam_init),
        grid=(b,),
        in_specs=in_specs,
        out_specs=new_spec,
        out_shape=jax.ShapeDtypeStruct((b, tq, w), BF16),
        compiler_params=_cparams(1),
        name=f"attn_{mode}_cache",
    )(*args)


def _out_kernel(x_ref, ma_ref, mb_ref, wa_ref, wb_ref, gf_ref, o_ref):
    hres = (x_ref[...]
            + jnp.dot(ma_ref[...], wa_ref[...], preferred_element_type=F32)
            + jnp.dot(mb_ref[...], wb_ref[...], preferred_element_type=F32))
    o_ref[...] = hres * _rms_scale(hres) * gf_ref[...]


def _out_call(x2, ma, mb, wa, wb, gf, tm):
    m, d = x2.shape
    da, db = ma.shape[1], mb.shape[1]
    return pl.pallas_call(
        _out_kernel,
        grid=(m // tm,),
        in_specs=[pl.BlockSpec((tm, d), lambda i: (i, 0)),
                  pl.BlockSpec((tm, da), lambda i: (i, 0)),
                  pl.BlockSpec((tm, db), lambda i: (i, 0)),
                  pl.BlockSpec((da, d), lambda i: (0, 0)),
                  pl.BlockSpec((db, d), lambda i: (0, 0)),
                  pl.BlockSpec((1, d), lambda i: (0, 0))],
        out_specs=pl.BlockSpec((tm, d), lambda i: (i, 0)),
        out_shape=jax.ShapeDtypeStruct((m, d), F32),
        compiler_params=_cparams(1),
        name="out_proj",
    )(x2, ma, mb, wa, wb, gf)


def _chunk_allowed(q, k):
    return (k // CHUNK) <= (q // CHUNK)


def _diff_tile_kq(t):
    k = jnp.arange(t)[:, None]
    q = jnp.arange(t)[None, :]
    return jnp.where(_chunk_allowed(q, k), jnp.minimum(0, 2 * (q - k)).astype(F32), NEG)


def _causal_tile_kq(t):
    k = jnp.arange(t)[:, None]
    q = jnp.arange(t)[None, :]
    return jnp.where(k <= q, 0.0, NEG).astype(F32)


def _diff_tile_qk(t):
    q = jnp.arange(t)[:, None]
    k = jnp.arange(t)[None, :]
    return jnp.where(_chunk_allowed(q, k), (q - jnp.abs(q - k)).astype(F32), NEG)


def _causal_tile_qk(t):
    return _causal_tile_kq(t).T


def _layer(x, past, params, *, tm, tq, cum_ch):
    (g_norm, w_bf, wf_hi, wf_lo, b_pad, slopes, lamv, subg, lam_init,
     wo_a, wo_b, g_final, d_a, nh_b) = params
    b, s, d = x.shape
    nh_a = d_a // HEAD
    x2 = x.reshape(b * s, d)
    q_scale_a = float((HEAD // 2) ** -0.5 * LOG2E)
    q_scale_b = float(HEAD ** -0.5 * LOG2E)

    prompt = past is None
    assert not prompt or tm == tq
    outs_a = _proj_call(x2, g_norm, w_bf, 0, d_a, q_scale_a, (2 * nh_a, HEAD // 2), tm, prompt)
    outs_b = _proj_call(x2, g_norm, w_bf, 1, d_a, q_scale_b, (d_a,), tm, prompt,
                        logf_args=(wf_hi, wf_lo, b_pad, nh_b))
    qa, ka, va, ga = outs_a[:4]
    qb, kb, vb, gb = outs_b[:4]
    logf = outs_b[-1]

    r3 = lambda a: a.reshape(b, s, a.shape[-1])
    if prompt:
        c = _cumsum_call(logf, s, cum_ch).reshape(b, s, nh_b)
        ma = _self_attn_call("diff", r3(qa), r3(outs_a[4]), outs_a[5], r3(ga), _diff_tile_kq(tq),
                             (slopes, lamv, subg), tq=tq, hp=1, lam_init=lam_init)
        mb = _self_attn_call("fox", r3(qb), r3(outs_b[4]), outs_b[5], r3(gb), _causal_tile_kq(tq),
                             c, tq=tq, hp=2)
    else:
        cak, cav, cbk, cbv, cbf = past
        lp = cak.shape[1]
        assert lp % CHUNK == 0 and s == tq <= CHUNK
        tot = lp + s
        padded = -(-tot // cum_ch) * cum_ch
        lf_all = jnp.concatenate([cbf, r3(logf), jnp.zeros((b, padded - tot, nh_b), F32)], axis=1)
        c = _cumsum_call(lf_all.reshape(b * padded, nh_b), padded, cum_ch)
        c = jnp.transpose(c.reshape(b, padded, nh_b), (0, 2, 1))
        slab = lambda a: a.reshape(b, lp * a.shape[2], HEAD)
        ma = _cache_attn_call("diff", r3(qa), cak, slab(cav), ka.reshape((b, s) + ka.shape[1:]),
                              r3(va), r3(ga), _diff_tile_qk(tq), (slopes, lamv, subg),
                              nh=nh_a, lp=lp, lam_init=lam_init)
        mb = _cache_attn_call("fox", r3(qb), slab(cbk), slab(cbv), r3(kb), r3(vb), r3(gb),
                              _causal_tile_qk(tq), (c[:, :, :lp], c[:, :, lp:tot]),
                              nh=nh_b, lp=lp)

    y = _out_call(x2, ma.reshape(b * s, -1), mb.reshape(b * s, -1), wo_a, wo_b, g_final,
                  min(tm, 512))
    return y.reshape(b, s, d), ka, va, kb, vb, logf


def kernel(x_prompt, x_sample, cache_a_k, cache_a_v, cache_b_k, cache_b_v, cache_b_logf,
           g_norm, w_in, b_f, lam_q1, lam_k1, lam_q2, lam_k2, sub_gain, w_out, g_final):
    depth, d = g_norm.shape
    assert depth == 1
    nh_a2, hqk = cache_a_k.shape[3], cache_a_k.shape[4]
    nh_a = nh_a2 // 2
    nh_b = cache_b_k.shape[3]
    d_a = nh_a2 * hqk
    assert 2 * hqk == HEAD and cache_b_k.shape[4] == HEAD and d == 2 * d_a
    lyr = 0
    lam_init = 0.8 - 0.6 * math.exp(-0.3 * lyr)

    w = w_in[lyr]
    w_bf = w.astype(BF16)
    wf = jnp.pad(w[:, 8 * d_a:], ((0, 0), (0, LANES - nh_b)))
    wf_hi = wf.astype(BF16)
    wf_lo = (wf - wf_hi.astype(F32)).astype(BF16)
    b_pad = jnp.pad(b_f[lyr].astype(F32), (0, LANES - nh_b)).reshape(1, LANES)
    slopes = 2.0 ** (-8.0 * jnp.arange(1, nh_a + 1, dtype=F32) / nh_a)
    lamv = jnp.pad(jnp.stack([lam_q1[lyr], lam_k1[lyr], lam_q2[lyr], lam_k2[lyr]]).astype(F32),
                   ((0, 0), (0, LANES - hqk)))
    subg = sub_gain[lyr].astype(F32).reshape(1, HEAD)
    wo = w_out[lyr].astype(BF16)
    params = (g_norm[lyr].reshape(1, d), w_bf, wf_hi, wf_lo, b_pad, slopes, lamv, subg, lam_init,
              wo[:d_a], wo[d_a:], g_final.reshape(1, d), d_a, nh_b)

    bp, sp, _ = x_prompt.shape
    bs, ss, _ = x_sample.shape
    yp, pak, pav, pbk, pbv, pbf = _layer(x_prompt, None, params, tm=512, tq=512, cum_ch=256)
    past = (cache_a_k[lyr], cache_a_v[lyr], cache_b_k[lyr], cache_b_v[lyr], cache_b_logf[lyr])
    ys, sak, sav, sbk, sbv, sbf = _layer(x_sample, past, params, tm=bs * ss, tq=ss, cum_ch=256)

    shp = lambda a, bb, s_, n, e: a.reshape(1, bb, s_, n, e)
    return (yp, ys,
            shp(pak, bp, sp, nh_a2, hqk), shp(pav, bp, sp, nh_a, HEAD),
            shp(pbk, bp, sp, nh_b, HEAD), shp(pbv, bp, sp, nh_b, HEAD),
            pbf.reshape(1, bp, sp, nh_b),
            shp(sak, bs, ss, nh_a2, hqk), shp(sav, bs, ss, nh_a, HEAD),
            shp(sbk, bs, ss, nh_b, HEAD), shp(sbv, bs, ss, nh_b, HEAD),
            sbf.reshape(1, bs, ss, nh_b))
```

```python
import functools
import math

import jax
import jax.numpy as jnp
from jax import lax
from jax.experimental import pallas as pl
from jax.experimental.pallas import tpu as pltpu

F32 = jnp.float32
BF16 = jnp.bfloat16

EPS = 1e-6
CHUNK = 64
HEAD = 128
NEG = -1e30
LANES = 128
LOG2E = math.log2(math.e)
N_BIAS = 3
QCOLS = 256
VMEM_LIMIT = 56 * 1024 * 1024

_NT = (((1,), (1,)), ((), ()))


def _cparams(n_axes):
    return pltpu.CompilerParams(
        dimension_semantics=("arbitrary",) * n_axes, vmem_limit_bytes=VMEM_LIMIT)


def _rms_scale(x):
    return lax.rsqrt(jnp.mean(x * x, axis=-1, keepdims=True) + EPS)


def _split3(x):
    x1 = x.astype(BF16)
    r1 = x - x1.astype(F32)
    x2 = r1.astype(BF16)
    x3 = (r1 - x2.astype(F32)).astype(BF16)
    return x1, x2, x3


def _proj_kernel(*refs, q_scale, width, k_minor, attn_copies, nh_logf):
    x_ref, g_ref, w_ref = refs[:3]
    n_out = 4 + (2 if attn_copies else 0) + (1 if nh_logf else 0)
    o_refs = refs[-n_out:]
    q_ref, k_ref, v_ref, gate_ref = o_refs[:4]
    x = x_ref[...]
    xn32 = x * _rms_scale(x) * g_ref[...]
    xn = xn32.astype(BF16)
    slab = lambda j: lax.dot_general(xn, w_ref[j * width:(j + 1) * width, :], _NT,
                                     preferred_element_type=F32)
    q_ref[...] = (slab(0) * q_scale).astype(q_ref.dtype)
    k = slab(1)
    k_ref[...] = k.reshape(k_ref.shape)
    v = slab(2)
    v_ref[...] = v
    gate_ref[...] = slab(3).astype(gate_ref.dtype)
    if attn_copies:
        kbf_ref, vt_ref = o_refs[4:6]
        kbf_ref[...] = k.astype(BF16)
        for h in range(width // HEAD):
            vt_ref[h, 0] = v[:, h * HEAD:(h + 1) * HEAD].T.astype(BF16)
    if nh_logf:
        whi_ref, wlo_ref, b_ref = refs[3:6]
        lo = (xn32 - xn.astype(F32)).astype(BF16)
        whi = whi_ref[...]
        fl = (lax.dot_general(xn, whi, _NT, preferred_element_type=F32)
              + lax.dot_general(xn, wlo_ref[...], _NT, preferred_element_type=F32)
              + lax.dot_general(lo, whi, _NT, preferred_element_type=F32))
        t = fl + b_ref[...]
        logf = jnp.minimum(t, 0.0) - jnp.log(1.0 + jnp.exp(-jnp.abs(t)))
        o_refs[-1][...] = logf[:, :nh_logf]


def _proj_call(x2, g, w_bf, group, width, q_scale, k_minor, tm, attn_copies, logf_args=None):
    m, d = x2.shape
    const = lambda i: (0, 0)
    row = lambda i: (i, 0)
    krow = lambda i: (i,) + (0,) * len(k_minor)
    in_specs = [pl.BlockSpec((tm, d), row),
                pl.BlockSpec((1, d), const),
                pl.BlockSpec((4 * width, d), lambda i: (group, 0), pipeline_mode=pl.Buffered(1))]
    args = [x2, g, w_bf]
    flat = pl.BlockSpec((tm, width), row)
    out_specs = [flat, pl.BlockSpec((tm,) + k_minor, krow), flat, flat]
    out_shape = [jax.ShapeDtypeStruct((m, width), BF16),
                 jax.ShapeDtypeStruct((m,) + k_minor, F32),
                 jax.ShapeDtypeStruct((m, width), F32),
                 jax.ShapeDtypeStruct((m, width), BF16)]
    if attn_copies:
        nh = width // HEAD
        out_specs += [flat, pl.BlockSpec((nh, 1, HEAD, tm), lambda i: (0, i, 0, 0))]
        out_shape += [jax.ShapeDtypeStruct((m, width), BF16),
                      jax.ShapeDtypeStruct((nh, m // tm, HEAD, tm), BF16)]
    nh_logf = 0
    if logf_args is not None:
        wf_hi, wf_lo, b_pad, nh_logf = logf_args
        in_specs += [pl.BlockSpec((LANES, d), const), pl.BlockSpec((LANES, d), const),
                     pl.BlockSpec((1, LANES), const)]
        args += [wf_hi, wf_lo, b_pad]
        out_specs.append(pl.BlockSpec((tm, nh_logf), row))
        out_shape.append(jax.ShapeDtypeStruct((m, nh_logf), F32))
    return pl.pallas_call(
        functools.partial(_proj_kernel, q_scale=q_scale, width=width, k_minor=k_minor,
                          attn_copies=attn_copies, nh_logf=nh_logf),
        grid=(m // tm,),
        in_specs=in_specs,
        out_specs=out_specs,
        out_shape=out_shape,
        compiler_params=_cparams(1),
        name=f"proj_group{group}",
    )(*args)


def _cumsum_kernel(x_ref, l_ref, o_ref, *, ch):
    ltri = l_ref[...]
    carry = jnp.zeros((1, x_ref.shape[1]), F32)
    for c in range(x_ref.shape[0] // ch):
        x1, x2, x3 = _split3(x_ref[c * ch:(c + 1) * ch, :])
        y = (jnp.dot(ltri, x1, preferred_element_type=F32)
             + jnp.dot(ltri, x2, preferred_element_type=F32)
             + jnp.dot(ltri, x3, preferred_element_type=F32)) + carry
        o_ref[c * ch:(c + 1) * ch, :] = y
        carry = y[ch - 1:ch, :]


def _cumsum_call(x, seq, ch):
    n, nh = x.shape
    tri = (jnp.arange(ch)[:, None] >= jnp.arange(ch)[None, :]).astype(BF16)
    return pl.pallas_call(
        functools.partial(_cumsum_kernel, ch=ch),
        grid=(n // seq,),
        in_specs=[pl.BlockSpec((seq, nh), lambda i: (i, 0)),
                  pl.BlockSpec((ch, ch), lambda i: (0, 0))],
        out_specs=pl.BlockSpec((seq, nh), lambda i: (i, 0)),
        out_shape=jax.ShapeDtypeStruct((n, nh), F32),
        compiler_params=_cparams(1),
        name="cumsum",
    )(x, tri)


def _lambda(lamv_ref, lam_init):
    lv = lamv_ref[...]
    return (jnp.exp(jnp.sum(lv[0:1] * lv[1:2], axis=1, keepdims=True))
            - jnp.exp(jnp.sum(lv[2:3] * lv[3:4], axis=1, keepdims=True)) + lam_init)


def _gate(g):
    g = g.astype(F32)
    return g / (1.0 + jnp.exp(-g))


def _stack_maps(q):
    lane = lax.broadcasted_iota(jnp.int32, q.shape, 1)
    zero = jnp.zeros_like(q)
    return jnp.concatenate([jnp.where(lane < HEAD // 2, q, zero),
                            jnp.where(lane >= HEAD // 2, q, zero)], axis=0)


def _self_attn_kernel(*refs, mode, tq, hp, lam_init):
    it = iter(refs)
    q_ref, k_ref, vt_ref, g_ref, tile_ref = next(it), next(it), next(it), next(it), next(it)
    if mode == "diff":
        slope_ref, lamv_ref, subg_ref = next(it), next(it), next(it)
    else:
        c_ref = next(it)
    o_ref = next(it)
    kaug_ref, qaug_ref, m_ref, l_ref, acc_ref, sa_ref, sb_ref = (next(it) for _ in range(7))

    hg = pl.program_id(1)
    nq = vt_ref.shape[1]
    tk = tq
    n = qaug_ref.shape[1]
    nhead = n // hp
    npairs = nq // 2
    if mode == "diff":
        bscale = slope_ref[hg] * LOG2E

    lane = lax.broadcasted_iota(jnp.int32, (tk, HEAD), 1)
    qlane = lax.broadcasted_iota(jnp.int32, (nhead, HEAD), 1)
    ones = jnp.where(qlane < N_BIAS, 1.0, 0.0).astype(BF16)
    for hs in range(hp):
        hl = slice(hs * HEAD, (hs + 1) * HEAD)
        for j in range(nq):
            rows = slice(j * tk, (j + 1) * tk)
            if mode == "diff":
                pos = lax.broadcasted_iota(jnp.int32, (tk, 1), 0) + j * tk
                bias = bscale * pos.astype(F32)
            else:
                c8 = c_ref[0, rows, :]
                sel = lax.broadcasted_iota(jnp.int32, c8.shape, 1) == hg * hp + hs
                bias = -LOG2E * jnp.sum(jnp.where(sel, c8, 0.0), axis=1, keepdims=True)
            b1, b2, b3 = (piece.astype(F32) for piece in _split3(bias))
            extra = jnp.where(lane == 0, b1, jnp.where(lane == 1, b2,
                                                       jnp.where(lane == 2, b3, 0.0)))
            kaug_ref[hs, rows, :HEAD] = k_ref[0, rows, hl]
            kaug_ref[hs, rows, HEAD:] = extra.astype(BF16)
            q = q_ref[0, rows, hl]
            qs = _stack_maps(q) if mode == "diff" else q
            qaug_ref[j, hs * nhead:(hs + 1) * nhead, :] = jnp.concatenate([qs, ones], axis=1)
    m_ref[...] = jnp.full(m_ref.shape, NEG, F32)
    l_ref[...] = jnp.zeros(l_ref.shape, F32)
    acc_ref[...] = jnp.zeros(acc_ref.shape, F32)

    groups = [(c0 // nhead, slice(c0, c0 + QCOLS)) for c0 in range(0, n, QCOLS)]

    tile = tile_ref[...]
    if mode == "diff":
        tile = bscale * tile

    def product(p, s):
        if isinstance(s, int) and s == nq - 1:
            return p, p
        if isinstance(s, int) and s == nq:
            return nq - 1 - p, nq - 1 - p
        first = s < p
        return jnp.where(first, p, nq - 1 - p), jnp.where(first, s, s - p)

    def is_diag(s):
        return isinstance(s, int) and s >= nq - 1

    def nkeys(s, group):
        low_queries = (group[1].start % nhead) % tq + QCOLS <= tq // 2
        return tk // 2 if is_diag(s) and low_queries else tk

    def scores(p, s, group):
        hs, cols = group
        qb, j = product(p, s)
        nk = nkeys(s, group)
        k0 = j * tk if isinstance(j, int) else pl.multiple_of(j * tk, tk)
        sc = lax.dot_general(kaug_ref[hs, pl.ds(k0, nk), :], qaug_ref[qb, cols, :], _NT,
                             preferred_element_type=F32)
        if is_diag(s):
            t0 = (cols.start % nhead) % tq
            sc = sc + tile[:nk, t0:t0 + QCOLS]
        return sc

    def consume(s_ref, p, s, group):
        hs, cols = group
        qb, j = product(p, s)
        nk = nkeys(s, group)
        sc = s_ref[:nk, cols]
        m_prev = m_ref[qb, :, cols]
        m_new = jnp.maximum(m_prev, jnp.max(sc, axis=0, keepdims=True))
        alpha = jnp.exp2(m_prev - m_new)
        pr = jnp.exp2(sc - m_new)
        l_ref[qb, :, cols] = alpha * l_ref[qb, :, cols] + jnp.sum(pr, axis=0, keepdims=True)
        acc_ref[qb, :, cols] = alpha * acc_ref[qb, :, cols] + jnp.dot(
            vt_ref[hs, j, :, :nk], pr.astype(BF16), preferred_element_type=F32)
        m_ref[qb, :, cols] = m_new

    def pipe_step(nxt, cur, nxt_ref, cur_ref):
        for group in groups:
            sc = scores(*nxt, group)
            nxt_ref[:sc.shape[0], group[1]] = sc
            if cur is not None:
                consume(cur_ref, *cur, group)

    def run_pair(p, a_ref, b_ref):
        def two_slots(i, carry):
            pipe_step((p, 2 * i + 1), (p, 2 * i), b_ref, a_ref)
            pipe_step((p, 2 * i + 2), (p, 2 * i + 1), a_ref, b_ref)
            return carry

        lax.fori_loop(0, (nq - 2) // 2, two_slots, 0)
        pipe_step((p, nq - 1), (p, nq - 2), b_ref, a_ref)
        pipe_step((p, nq), (p, nq - 1), a_ref, b_ref)
        pipe_step((jnp.minimum(p + 1, npairs - 1), 0), (p, nq), b_ref, a_ref)

    pipe_step((0, 0), None, sa_ref, None)

    def two_pairs(i, carry):
        run_pair(2 * i, sa_ref, sb_ref)
        run_pair(2 * i + 1, sb_ref, sa_ref)
        return carry

    lax.fori_loop(0, npairs // 2, two_pairs, 0)

    def finalize_block(blk):
        rows = pl.ds(pl.multiple_of(blk * tq, tq), tq)
        for hs in range(hp):
            hl = slice(hs * HEAD, (hs + 1) * HEAD)
            hc = slice(hs * nhead, (hs + 1) * nhead)
            ot = acc_ref[blk, :, hc] * (1.0 / l_ref[blk, :, hc])
            if mode == "diff":
                ot = ot[:, :tq] - _lambda(lamv_ref, lam_init) * ot[:, tq:]
            o = ot.T
            if mode == "diff":
                o = o * _rms_scale(o) * subg_ref[...] * (1.0 - lam_init)
            o_ref[0, rows, hl] = (_gate(g_ref[0, rows, hl]) * o).astype(o_ref.dtype)

    def finalize_two(i, carry):
        finalize_block(2 * i)
        finalize_block(2 * i + 1)
        return carry

    lax.fori_loop(0, nq // 2, finalize_two, 0)


def _self_attn_call(mode, q, k, vt, g, tile, extra, *, tq, hp, lam_init=0.0):
    b, t, w = q.shape
    nh = w // HEAD
    nq = t // tq
    assert nq % 4 == 0 and nh % hp == 0 and (mode != "diff" or hp == 1)
    assert vt.shape == (nh, b * nq, HEAD, tq)
    n = hp * (2 * tq if mode == "diff" else tq)
    seq_spec = pl.BlockSpec((1, t, hp * HEAD), lambda bi, hi: (bi, 0, hi))
    const2 = lambda bi, hi: (0, 0)
    in_specs = [seq_spec, seq_spec,
                pl.BlockSpec((hp, nq, HEAD, tq), lambda bi, hi: (hi, bi, 0, 0)),
                seq_spec,
                pl.BlockSpec((tq, tq), const2, pipeline_mode=pl.Buffered(1))]
    args = [q, k, vt, g, tile]
    if mode == "diff":
        slopes, lamv, subg = extra
        in_specs += [pl.BlockSpec(memory_space=pltpu.SMEM),
                     pl.BlockSpec((4, LANES), const2),
                     pl.BlockSpec((1, HEAD), const2)]
        args += [slopes, lamv, subg]
    else:
        c = extra
        in_specs += [pl.BlockSpec((1, t, c.shape[2]), lambda bi, hi: (bi, 0, 0),
                                  pipeline_mode=pl.Buffered(1))]
        args += [c]
    return pl.pallas_call(
        functools.partial(_self_attn_kernel, mode=mode, tq=tq, hp=hp, lam_init=lam_init),
        grid=(b, nh // hp),
        in_specs=in_specs,
        out_specs=seq_spec,
        out_shape=jax.ShapeDtypeStruct((b, t, w), BF16),
        scratch_shapes=[pltpu.VMEM((hp, t, 2 * HEAD), BF16),
                        pltpu.VMEM((nq, n, 2 * HEAD), BF16),
                        pltpu.VMEM((nq, 1, n), F32), pltpu.VMEM((nq, 1, n), F32),
                        pltpu.VMEM((nq, HEAD, n), F32),
                        pltpu.VMEM((tq, n), F32), pltpu.VMEM((tq, n), F32)],
        compiler_params=_cparams(2),
        name=f"attn_{mode}_self",
    )(*args)


def _cache_attn_kernel(*refs, mode, nh, lp, lam_init):
    it = iter(refs)
    q_ref, kc_ref, vc_ref, kn_ref, vn_ref, g_ref, tile_ref = (next(it) for _ in range(7))
    if mode == "diff":
        slope_ref, lamv_ref, subg_ref = next(it), next(it), next(it)
    else:
        cp_ref, cd_ref = next(it), next(it)
    o_ref = next(it)
    tq = q_ref.shape[1]
    tile = tile_ref[...]
    col = lax.broadcasted_iota(jnp.int32, (1, lp), 1).astype(F32)
    if mode == "diff":
        kn_all = kn_ref[0].reshape(tq, nh * HEAD).astype(BF16)

    for h in range(nh):
        lanes = slice(h * HEAD, (h + 1) * HEAD)
        heads = pl.ds(h, lp, stride=nh)
        q = q_ref[0, :, lanes]
        vn = vn_ref[0, :, lanes].astype(BF16)
        vc = vc_ref[0, heads, :].astype(BF16)
        if mode == "diff":
            qs = _stack_maps(q)
            kc_t = kc_ref[0, 2 * h:2 * h + 2].reshape(HEAD, lp).astype(BF16)
            kn = kn_all[:, lanes]
            bscale = slope_ref[h] * LOG2E
            bias_c = bscale * (col - float(lp))
            bias_n = bscale * jnp.concatenate([tile, tile], axis=0)
        else:
            qs = q
            kc = kc_ref[0, heads, :].astype(BF16)
            kn = kn_ref[0, :, lanes].astype(BF16)
            cn = cd_ref[0, h:h + 1, :]
            cref = cn[:, 0:1]
            bias_c = LOG2E * (cref - cp_ref[0, h:h + 1, :])
            bias_n = tile + LOG2E * (cref - cn)
        if mode == "diff":
            s_c = jnp.dot(qs, kc_t, preferred_element_type=F32) + bias_c
        else:
            s_c = lax.dot_general(qs, kc, _NT, preferred_element_type=F32) + bias_c
        s_n = lax.dot_general(qs, kn, _NT, preferred_element_type=F32) + bias_n
        m = jnp.maximum(jnp.max(s_c, axis=1, keepdims=True), jnp.max(s_n, axis=1, keepdims=True))
        p_c = jnp.exp2(s_c - m)
        p_n = jnp.exp2(s_n - m)
        l = jnp.sum(p_c, axis=1, keepdims=True) + jnp.sum(p_n, axis=1, keepdims=True)
        acc = (jnp.dot(p_c.astype(BF16), vc, preferred_element_type=F32)
               + jnp.dot(p_n.astype(BF16), vn, preferred_element_type=F32))
        o = acc / l
        if mode == "diff":
            o = o[:tq] - _lambda(lamv_ref, lam_init) * o[tq:]
            o = o * _rms_scale(o) * subg_ref[...] * (1.0 - lam_init)
        o_ref[0, :, lanes] = (_gate(g_ref[0, :, lanes]) * o).astype(o_ref.dtype)


def _cache_attn_call(mode, q, kc, vc, kn, vn, g, tile, extra, *, nh, lp, lam_init=0.0):
    b, tq, w = q.shape
    row3 = lambda bi: (bi, 0, 0)
    const2 = lambda bi: (0, 0)
    new_spec = pl.BlockSpec((1, tq, w), row3)
    whole = lambda a: pl.BlockSpec((1,) + a.shape[1:], lambda bi: (bi,) + (0,) * (a.ndim - 1))
    in_specs = [new_spec, whole(kc), whole(vc), whole(kn), new_spec, new_spec,
                pl.BlockSpec((tq, tq), const2)]
    args = [q, kc, vc, kn, vn, g, tile]
    if mode == "diff":
        slopes, lamv, subg = extra
        in_specs += [pl.BlockSpec(memory_space=pltpu.SMEM),
                     pl.BlockSpec((4, LANES), const2),
                     pl.BlockSpec((1, HEAD), const2)]
        args += [slopes, lamv, subg]
    else:
        cp, cd = extra
        in_specs += [pl.BlockSpec((1,) + cp.shape[1:], row3),
                     pl.BlockSpec((1,) + cd.shape[1:], row3)]
        args += [cp, cd]
    return pl.pallas_call(
        functools.partial(_cache_attn_kernel, mode=mode, nh=nh, lp=lp, lam_init=lam_init),
        grid=(b,),
        in_specs=in_specs,
        out_specs=new_spec,
        out_shape=jax.ShapeDtypeStruct((b, tq, w), BF16),
        compiler_params=_cparams(1),
        name=f"attn_{mode}_cache",
    )(*args)


def _out_kernel(x_ref, ma_ref, mb_ref, wa_ref, wb_ref, gf_ref, o_ref):
    hres = (x_ref[...]
            + jnp.dot(ma_ref[...], wa_ref[...], preferred_element_type=F32)
            + jnp.dot(mb_ref[...], wb_ref[...], preferred_element_type=F32))
    o_ref[...] = hres * _rms_scale(hres) * gf_ref[...]


def _out_call(x2, ma, mb, wa, wb, gf, tm):
    m, d = x2.shape
    da, db = ma.shape[1], mb.shape[1]
    return pl.pallas_call(
        _out_kernel,
        grid=(m // tm,),
        in_specs=[pl.BlockSpec((tm, d), lambda i: (i, 0)),
                  pl.BlockSpec((tm, da), lambda i: (i, 0)),
                  pl.BlockSpec((tm, db), lambda i: (i, 0)),
                  pl.BlockSpec((da, d), lambda i: (0, 0)),
                  pl.BlockSpec((db, d), lambda i: (0, 0)),
                  pl.BlockSpec((1, d), lambda i: (0, 0))],
        out_specs=pl.BlockSpec((tm, d), lambda i: (i, 0)),
        out_shape=jax.ShapeDtypeStruct((m, d), F32),
        compiler_params=_cparams(1),
        name="out_proj",
    )(x2, ma, mb, wa, wb, gf)


def _chunk_allowed(q, k):
    return (k // CHUNK) <= (q // CHUNK)


def _diff_tile_kq(t):
    k = jnp.arange(t)[:, None]
    q = jnp.arange(t)[None, :]
    return jnp.where(_chunk_allowed(q, k), jnp.minimum(0, 2 * (q - k)).astype(F32), NEG)


def _causal_tile_kq(t):
    k = jnp.arange(t)[:, None]
    q = jnp.arange(t)[None, :]
    return jnp.where(k <= q, 0.0, NEG).astype(F32)


def _diff_tile_qk(t):
    q = jnp.arange(t)[:, None]
    k = jnp.arange(t)[None, :]
    return jnp.where(_chunk_allowed(q, k), (q - jnp.abs(q - k)).astype(F32), NEG)


def _causal_tile_qk(t):
    return _causal_tile_kq(t).T


def _layer(x, past, params, *, tm, tq, cum_ch):
    (g_norm, w_bf, wf_hi, wf_lo, b_pad, slopes, lamv, subg, lam_init,
     wo_a, wo_b, g_final, d_a, nh_b) = params
    b, s, d = x.shape
    nh_a = d_a // HEAD
    x2 = x.reshape(b * s, d)
    q_scale_a = float((HEAD // 2) ** -0.5 * LOG2E)
    q_scale_b = float(HEAD ** -0.5 * LOG2E)

    prompt = past is None
    assert not prompt or tm == tq
    outs_a = _proj_call(x2, g_norm, w_bf, 0, d_a, q_scale_a, (2 * nh_a, HEAD // 2), tm, prompt)
    outs_b = _proj_call(x2, g_norm, w_bf, 1, d_a, q_scale_b, (d_a,), tm, prompt,
                        logf_args=(wf_hi, wf_lo, b_pad, nh_b))
    qa, ka, va, ga = outs_a[:4]
    qb, kb, vb, gb = outs_b[:4]
    logf = outs_b[-1]

    r3 = lambda a: a.reshape(b, s, a.shape[-1])
    if prompt:
        c = _cumsum_call(logf, s, cum_ch).reshape(b, s, nh_b)
        ma = _self_attn_call("diff", r3(qa), r3(outs_a[4]), outs_a[5], r3(ga), _diff_tile_kq(tq),
                             (slopes, lamv, subg), tq=tq, hp=1, lam_init=lam_init)
        mb = _self_attn_call("fox", r3(qb), r3(outs_b[4]), outs_b[5], r3(gb), _causal_tile_kq(tq),
                             c, tq=tq, hp=2)
    else:
        cak, cav, cbk, cbv, cbf = past
        lp = cak.shape[1]
        assert lp % CHUNK == 0 and s == tq <= CHUNK
        tot = lp + s
        padded = -(-tot // cum_ch) * cum_ch
        lf_all = jnp.concatenate([cbf, r3(logf), jnp.zeros((b, padded - tot, nh_b), F32)], axis=1)
        c = _cumsum_call(lf_all.reshape(b * padded, nh_b), padded, cum_ch)
        c = jnp.transpose(c.reshape(b, padded, nh_b), (0, 2, 1))
        slab = lambda a: a.reshape(b, lp * a.shape[2], HEAD)
        cak_t = jnp.transpose(cak, (0, 2, 3, 1))
        ma = _cache_attn_call("diff", r3(qa), cak_t, slab(cav), ka.reshape((b, s) + ka.shape[1:]),
                              r3(va), r3(ga), _diff_tile_qk(tq), (slopes, lamv, subg),
                              nh=nh_a, lp=lp, lam_init=lam_init)
        mb = _cache_attn_call("fox", r3(qb), slab(cbk), slab(cbv), r3(kb), r3(vb), r3(gb),
                              _causal_tile_qk(tq), (c[:, :, :lp], c[:, :, lp:tot]),
                              nh=nh_b, lp=lp)

    y = _out_call(x2, ma.reshape(b * s, -1), mb.reshape(b * s, -1), wo_a, wo_b, g_final,
                  min(tm, 512))
    return y.reshape(b, s, d), ka, va, kb, vb, logf


def kernel(x_prompt, x_sample, cache_a_k, cache_a_v, cache_b_k, cache_b_v, cache_b_logf,
           g_norm, w_in, b_f, lam_q1, lam_k1, lam_q2, lam_k2, sub_gain, w_out, g_final):
    depth, d = g_norm.shape
    assert depth == 1
    nh_a2, hqk = cache_a_k.shape[3], cache_a_k.shape[4]
    nh_a = nh_a2 // 2
    nh_b = cache_b_k.shape[3]
    d_a = nh_a2 * hqk
    assert 2 * hqk == HEAD and cache_b_k.shape[4] == HEAD and d == 2 * d_a
    lyr = 0
    lam_init = 0.8 - 0.6 * math.exp(-0.3 * lyr)

    w = jnp.transpose(w_in[lyr])
    w_bf = w.astype(BF16)
    wf = jnp.pad(w[8 * d_a:], ((0, LANES - nh_b), (0, 0)))
    wf_hi = wf.astype(BF16)
    wf_lo = (wf - wf_hi.astype(F32)).astype(BF16)
    b_pad = jnp.pad(b_f[lyr].astype(F32), (0, LANES - nh_b)).reshape(1, LANES)
    slopes = 2.0 ** (-8.0 * jnp.arange(1, nh_a + 1, dtype=F32) / nh_a)
    lamv = jnp.pad(jnp.stack([lam_q1[lyr], lam_k1[lyr], lam_q2[lyr], lam_k2[lyr]]).astype(F32),
                   ((0, 0), (0, LANES - hqk)))
    subg = sub_gain[lyr].astype(F32).reshape(1, HEAD)
    wo = w_out[lyr].astype(BF16)
    params = (g_norm[lyr].reshape(1, d), w_bf, wf_hi, wf_lo, b_pad, slopes, lamv, subg, lam_init,
              wo[:d_a], wo[d_a:], g_final.reshape(1, d), d_a, nh_b)

    bp, sp, _ = x_prompt.shape
    bs, ss, _ = x_sample.shape
    yp, pak, pav, pbk, pbv, pbf = _layer(x_prompt, None, params, tm=512, tq=512, cum_ch=256)
    past = (cache_a_k[lyr], cache_a_v[lyr], cache_b_k[lyr], cache_b_v[lyr], cache_b_logf[lyr])
    ys, sak, sav, sbk, sbv, sbf = _layer(x_sample, past, params, tm=bs * ss, tq=ss, cum_ch=256)

    shp = lambda a, bb, s_, n, e: a.reshape(1, bb, s_, n, e)
    return (yp, ys,
            shp(pak, bp, sp, nh_a2, hqk), shp(pav, bp, sp, nh_a, HEAD),
            shp(pbk, bp, sp, nh_b, HEAD), shp(pbv, bp, sp, nh_b, HEAD),
            pbf.reshape(1, bp, sp, nh_b),
            shp(sak, bs, ss, nh_a2, hqk), shp(sav, bs, ss, nh_a, HEAD),
            shp(sbk, bs, ss, nh_b, HEAD), shp(sbv, bs, ss, nh_b, HEAD),
            sbf.reshape(1, bs, ss, nh_b))
```

```python
import functools
import math

import jax
import jax.numpy as jnp
from jax import lax
from jax.experimental import pallas as pl
from jax.experimental.pallas import tpu as pltpu

F32 = jnp.float32
BF16 = jnp.bfloat16

EPS = 1e-6
CHUNK = 64
HEAD = 128
NEG = -1e30
LANES = 128
LOG2E = math.log2(math.e)
N_BIAS = 3
QCOLS = 256
VMEM_LIMIT = 56 * 1024 * 1024

_NT = (((1,), (1,)), ((), ()))


def _cparams(n_axes):
    return pltpu.CompilerParams(
        dimension_semantics=("arbitrary",) * n_axes, vmem_limit_bytes=VMEM_LIMIT)


def _rms_scale(x):
    return lax.rsqrt(jnp.mean(x * x, axis=-1, keepdims=True) + EPS)


def _split3(x):
    x1 = x.astype(BF16)
    r1 = x - x1.astype(F32)
    x2 = r1.astype(BF16)
    x3 = (r1 - x2.astype(F32)).astype(BF16)
    return x1, x2, x3


def _proj_kernel(*refs, q_scale, width, k_minor, attn_copies, nh_logf):
    x_ref, g_ref, w_ref = refs[:3]
    n_out = 4 + (2 if attn_copies else 0) + (1 if nh_logf else 0)
    o_refs = refs[-n_out:]
    q_ref, k_ref, v_ref, gate_ref = o_refs[:4]
    x = x_ref[...]
    xn32 = x * _rms_scale(x) * g_ref[...]
    xn = xn32.astype(BF16)
    slab = lambda j: lax.dot_general(xn, w_ref[j * width:(j + 1) * width, :], _NT,
                                     preferred_element_type=F32)
    q_ref[...] = (slab(0) * q_scale).astype(q_ref.dtype)
    k = slab(1)
    k_ref[...] = k.reshape(k_ref.shape)
    v = slab(2)
    v_ref[...] = v
    gate_ref[...] = slab(3).astype(gate_ref.dtype)
    if attn_copies:
        kbf_ref, vt_ref = o_refs[4:6]
        kbf_ref[...] = k.astype(BF16)
        for h in range(width // HEAD):
            vt_ref[h, 0] = v[:, h * HEAD:(h + 1) * HEAD].T.astype(BF16)
    if nh_logf:
        whl_ref, b_ref = refs[3:5]
        lo = (xn32 - xn.astype(F32)).astype(BF16)
        both = lax.dot_general(xn, whl_ref[...], _NT, preferred_element_type=F32)
        fl = (both[:, :LANES] + both[:, LANES:]
              + lax.dot_general(lo, whl_ref[:LANES, :], _NT, preferred_element_type=F32))
        t = fl + b_ref[...]
        logf = jnp.minimum(t, 0.0) - jnp.log(1.0 + jnp.exp(-jnp.abs(t)))
        o_refs[-1][...] = logf[:, :nh_logf]


def _proj_call(x2, g, w_bf, group, width, q_scale, k_minor, tm, attn_copies, logf_args=None):
    m, d = x2.shape
    const = lambda i: (0, 0)
    row = lambda i: (i, 0)
    krow = lambda i: (i,) + (0,) * len(k_minor)
    in_specs = [pl.BlockSpec((tm, d), row),
                pl.BlockSpec((1, d), const),
                pl.BlockSpec((4 * width, d), lambda i: (group, 0), pipeline_mode=pl.Buffered(1))]
    args = [x2, g, w_bf]
    flat = pl.BlockSpec((tm, width), row)
    out_specs = [flat, pl.BlockSpec((tm,) + k_minor, krow), flat, flat]
    out_shape = [jax.ShapeDtypeStruct((m, width), BF16),
                 jax.ShapeDtypeStruct((m,) + k_minor, F32),
                 jax.ShapeDtypeStruct((m, width), F32),
                 jax.ShapeDtypeStruct((m, width), BF16)]
    if attn_copies:
        nh = width // HEAD
        out_specs += [flat, pl.BlockSpec((nh, 1, HEAD, tm), lambda i: (0, i, 0, 0))]
        out_shape += [jax.ShapeDtypeStruct((m, width), BF16),
                      jax.ShapeDtypeStruct((nh, m // tm, HEAD, tm), BF16)]
    nh_logf = 0
    if logf_args is not None:
        wf_hl, b_pad, nh_logf = logf_args
        in_specs += [pl.BlockSpec((2 * LANES, d), const), pl.BlockSpec((1, LANES), const)]
        args += [wf_hl, b_pad]
        out_specs.append(pl.BlockSpec((tm, nh_logf), row))
        out_shape.append(jax.ShapeDtypeStruct((m, nh_logf), F32))
    return pl.pallas_call(
        functools.partial(_proj_kernel, q_scale=q_scale, width=width, k_minor=k_minor,
                          attn_copies=attn_copies, nh_logf=nh_logf),
        grid=(m // tm,),
        in_specs=in_specs,
        out_specs=out_specs,
        out_shape=out_shape,
        compiler_params=_cparams(1),
        name=f"proj_group{group}",
    )(*args)


def _cumsum_kernel(x_ref, l_ref, o_ref, *, ch):
    ltri = l_ref[...]
    carry = jnp.zeros((1, x_ref.shape[1]), F32)
    for c in range(x_ref.shape[0] // ch):
        x1, x2, x3 = _split3(x_ref[c * ch:(c + 1) * ch, :])
        y = (jnp.dot(ltri, x1, preferred_element_type=F32)
             + jnp.dot(ltri, x2, preferred_element_type=F32)
             + jnp.dot(ltri, x3, preferred_element_type=F32)) + carry
        o_ref[c * ch:(c + 1) * ch, :] = y
        carry = y[ch - 1:ch, :]


def _cumsum_call(x, seq, ch):
    n, nh = x.shape
    tri = (jnp.arange(ch)[:, None] >= jnp.arange(ch)[None, :]).astype(BF16)
    return pl.pallas_call(
        functools.partial(_cumsum_kernel, ch=ch),
        grid=(n // seq,),
        in_specs=[pl.BlockSpec((seq, nh), lambda i: (i, 0)),
                  pl.BlockSpec((ch, ch), lambda i: (0, 0))],
        out_specs=pl.BlockSpec((seq, nh), lambda i: (i, 0)),
        out_shape=jax.ShapeDtypeStruct((n, nh), F32),
        compiler_params=_cparams(1),
        name="cumsum",
    )(x, tri)


def _lambda(lamv_ref, lam_init):
    lv = lamv_ref[...]
    return (jnp.exp(jnp.sum(lv[0:1] * lv[1:2], axis=1, keepdims=True))
            - jnp.exp(jnp.sum(lv[2:3] * lv[3:4], axis=1, keepdims=True)) + lam_init)


def _gate(g):
    g = g.astype(F32)
    return g / (1.0 + jnp.exp(-g))


def _stack_maps(q):
    lane = lax.broadcasted_iota(jnp.int32, q.shape, 1)
    zero = jnp.zeros_like(q)
    return jnp.concatenate([jnp.where(lane < HEAD // 2, q, zero),
                            jnp.where(lane >= HEAD // 2, q, zero)], axis=0)


def _self_attn_kernel(*refs, mode, tq, hp, lam_init):
    it = iter(refs)
    q_ref, k_ref, vt_ref, g_ref, tile_ref = next(it), next(it), next(it), next(it), next(it)
    if mode == "diff":
        slope_ref, lamv_ref, subg_ref = next(it), next(it), next(it)
    else:
        c_ref = next(it)
    o_ref = next(it)
    kaug_ref, qaug_ref, m_ref, l_ref, acc_ref, sa_ref, sb_ref = (next(it) for _ in range(7))

    hg = pl.program_id(1)
    nq = vt_ref.shape[1]
    tk = tq
    n = qaug_ref.shape[1]
    nhead = n // hp
    npairs = nq // 2
    if mode == "diff":
        bscales = [slope_ref[hg * hp + hs] * LOG2E for hs in range(hp)]

    lane = lax.broadcasted_iota(jnp.int32, (tk, HEAD), 1)
    qlane = lax.broadcasted_iota(jnp.int32, (nhead, HEAD), 1)
    for hs in range(hp):
        if mode == "diff":
            s1, s2, s3 = (piece.astype(F32)
                          for piece in _split3(jnp.full((1, HEAD), bscales[hs], F32)))
            ones = jnp.where(qlane < 2, s1, jnp.where(qlane < 4, s2,
                                                      jnp.where(qlane < 6, s3, 0.0))).astype(BF16)
        else:
            ones = jnp.where(qlane < N_BIAS, 1.0, 0.0).astype(BF16)
        hl = slice(hs * HEAD, (hs + 1) * HEAD)
        for j in range(nq):
            rows = slice(j * tk, (j + 1) * tk)
            if mode == "diff":
                pos = lax.broadcasted_iota(jnp.int32, (tk, HEAD), 0) + j * tk
                piece = jnp.where(lane % 2 == 0, (pos // 16) * 16, pos % 16)
                extra = jnp.where(lane < 6, piece, 0).astype(F32)
            else:
                c8 = c_ref[0, rows, :]
                sel = lax.broadcasted_iota(jnp.int32, c8.shape, 1) == hg * hp + hs
                bias = -LOG2E * jnp.sum(jnp.where(sel, c8, 0.0), axis=1, keepdims=True)
                b1, b2, b3 = (piece.astype(F32) for piece in _split3(bias))
                extra = jnp.where(lane == 0, b1, jnp.where(lane == 1, b2,
                                                           jnp.where(lane == 2, b3, 0.0)))
            kaug_ref[hs, rows, :HEAD] = k_ref[0, rows, hl]
            kaug_ref[hs, rows, HEAD:] = extra.astype(BF16)
            q = q_ref[0, rows, hl]
            qs = _stack_maps(q) if mode == "diff" else q
            qaug_ref[j, hs * nhead:(hs + 1) * nhead, :] = jnp.concatenate([qs, ones], axis=1)
    m_ref[...] = jnp.full(m_ref.shape, NEG, F32)
    l_ref[...] = jnp.zeros(l_ref.shape, F32)
    acc_ref[...] = jnp.zeros(acc_ref.shape, F32)

    groups = [(c0 // nhead, slice(c0, c0 + QCOLS)) for c0 in range(0, n, QCOLS)]

    tile = tile_ref[...]
    tiles = [bscales[hs] * tile if mode == "diff" else tile for hs in range(hp)]

    def product(p, s):
        if isinstance(s, int) and s == nq - 1:
            return p, p
        if isinstance(s, int) and s == nq:
            return nq - 1 - p, nq - 1 - p
        first = s < p
        return jnp.where(first, p, nq - 1 - p), jnp.where(first, s, s - p)

    def is_diag(s):
        return isinstance(s, int) and s >= nq - 1

    def nkeys(s, group):
        low_queries = (group[1].start % nhead) % tq + QCOLS <= tq // 2
        return tk // 2 if is_diag(s) and low_queries else tk

    def scores(p, s, group):
        hs, cols = group
        qb, j = product(p, s)
        nk = nkeys(s, group)
        k0 = j * tk if isinstance(j, int) else pl.multiple_of(j * tk, tk)
        sc = lax.dot_general(kaug_ref[hs, pl.ds(k0, nk), :], qaug_ref[qb, cols, :], _NT,
                             preferred_element_type=F32)
        if is_diag(s):
            t0 = (cols.start % nhead) % tq
            sc = sc + tiles[hs][:nk, t0:t0 + QCOLS]
        return sc

    def consume(s_ref, p, s, group):
        hs, cols = group
        qb, j = product(p, s)
        nk = nkeys(s, group)
        sc = s_ref[:nk, cols]
        m_prev = m_ref[qb, :, cols]
        m_new = jnp.maximum(m_prev, jnp.max(sc, axis=0, keepdims=True))
        alpha = jnp.exp2(m_prev - m_new)
        pr = jnp.exp2(sc - m_new)
        l_ref[qb, :, cols] = alpha * l_ref[qb, :, cols] + jnp.sum(pr, axis=0, keepdims=True)
        acc_ref[qb, :, cols] = alpha * acc_ref[qb, :, cols] + jnp.dot(
            vt_ref[hs, j, :, :nk], pr.astype(BF16), preferred_element_type=F32)
        m_ref[qb, :, cols] = m_new

    def pipe_step(nxt, cur, nxt_ref, cur_ref):
        for group in groups:
            sc = scores(*nxt, group)
            nxt_ref[:sc.shape[0], group[1]] = sc
            if cur is not None:
                consume(cur_ref, *cur, group)

    def run_pair(p, a_ref, b_ref):
        def two_slots(i, carry):
            pipe_step((p, 2 * i + 1), (p, 2 * i), b_ref, a_ref)
            pipe_step((p, 2 * i + 2), (p, 2 * i + 1), a_ref, b_ref)
            return carry

        lax.fori_loop(0, (nq - 2) // 2, two_slots, 0)
        pipe_step((p, nq - 1), (p, nq - 2), b_ref, a_ref)
        pipe_step((p, nq), (p, nq - 1), a_ref, b_ref)
        pipe_step((jnp.minimum(p + 1, npairs - 1), 0), (p, nq), b_ref, a_ref)

    pipe_step((0, 0), None, sa_ref, None)

    def two_pairs(i, carry):
        run_pair(2 * i, sa_ref, sb_ref)
        run_pair(2 * i + 1, sb_ref, sa_ref)
        return carry

    lax.fori_loop(0, npairs // 2, two_pairs, 0)

    def finalize_block(blk):
        rows = pl.ds(pl.multiple_of(blk * tq, tq), tq)
        for hs in range(hp):
            hl = slice(hs * HEAD, (hs + 1) * HEAD)
            hc = slice(hs * nhead, (hs + 1) * nhead)
            ot = acc_ref[blk, :, hc] * (1.0 / l_ref[blk, :, hc])
            if mode == "diff":
                ot = ot[:, :tq] - _lambda(lamv_ref, lam_init) * ot[:, tq:]
            o = ot.T
            if mode == "diff":
                o = o * _rms_scale(o) * subg_ref[...] * (1.0 - lam_init)
            o_ref[0, rows, hl] = (_gate(g_ref[0, rows, hl]) * o).astype(o_ref.dtype)

    def finalize_two(i, carry):
        finalize_block(2 * i)
        finalize_block(2 * i + 1)
        return carry

    lax.fori_loop(0, nq // 2, finalize_two, 0)


def _self_attn_call(mode, q, k, vt, g, tile, extra, *, tq, hp, lam_init=0.0):
    b, t, w = q.shape
    nh = w // HEAD
    nq = t // tq
    assert nq % 4 == 0 and nh % hp == 0
    assert vt.shape == (nh, b * nq, HEAD, tq)
    n = hp * (2 * tq if mode == "diff" else tq)
    seq_spec = pl.BlockSpec((1, t, hp * HEAD), lambda bi, hi: (bi, 0, hi))
    const2 = lambda bi, hi: (0, 0)
    in_specs = [seq_spec, seq_spec,
                pl.BlockSpec((hp, nq, HEAD, tq), lambda bi, hi: (hi, bi, 0, 0)),
                seq_spec,
                pl.BlockSpec((tq, tq), const2, pipeline_mode=pl.Buffered(1))]
    args = [q, k, vt, g, tile]
    if mode == "diff":
        slopes, lamv, subg = extra
        in_specs += [pl.BlockSpec(memory_space=pltpu.SMEM),
                     pl.BlockSpec((4, LANES), const2),
                     pl.BlockSpec((1, HEAD), const2)]
        args += [slopes, lamv, subg]
    else:
        c = extra
        in_specs += [pl.BlockSpec((1, t, c.shape[2]), lambda bi, hi: (bi, 0, 0),
                                  pipeline_mode=pl.Buffered(1))]
        args += [c]
    return pl.pallas_call(
        functools.partial(_self_attn_kernel, mode=mode, tq=tq, hp=hp, lam_init=lam_init),
        grid=(b, nh // hp),
        in_specs=in_specs,
        out_specs=seq_spec,
        out_shape=jax.ShapeDtypeStruct((b, t, w), BF16),
        scratch_shapes=[pltpu.VMEM((hp, t, 2 * HEAD), BF16),
                        pltpu.VMEM((nq, n, 2 * HEAD), BF16),
                        pltpu.VMEM((nq, 1, n), F32), pltpu.VMEM((nq, 1, n), F32),
                        pltpu.VMEM((nq, HEAD, n), F32),
                        pltpu.VMEM((tq, n), F32), pltpu.VMEM((tq, n), F32)],
        compiler_params=_cparams(2),
        name=f"attn_{mode}_self",
    )(*args)


def _cache_attn_kernel(*refs, mode, nh, lp, lam_init):
    it = iter(refs)
    q_ref, kc_ref, vc_ref, kn_ref, vn_ref, g_ref, tile_ref = (next(it) for _ in range(7))
    if mode == "diff":
        slope_ref, lamv_ref, subg_ref = next(it), next(it), next(it)
    else:
        cp_ref, cd_ref = next(it), next(it)
    o_ref = next(it)
    tq = q_ref.shape[1]
    tile = tile_ref[...]
    col = lax.broadcasted_iota(jnp.int32, (1, lp), 1).astype(F32)
    if mode == "diff":
        kn_all = kn_ref[0].reshape(tq, nh * HEAD).astype(BF16)

    for h in range(nh):
        lanes = slice(h * HEAD, (h + 1) * HEAD)
        heads = pl.ds(h, lp, stride=nh)
        q = q_ref[0, :, lanes]
        vn = vn_ref[0, :, lanes].astype(BF16)
        vc = vc_ref[0, heads, :].astype(BF16)
        if mode == "diff":
            qs = _stack_maps(q)
            kc_t = kc_ref[0, 2 * h:2 * h + 2].reshape(HEAD, lp).astype(BF16)
            kn = kn_all[:, lanes]
            bscale = slope_ref[h] * LOG2E
            bias_c = bscale * (col - float(lp))
            bias_n = bscale * jnp.concatenate([tile, tile], axis=0)
        else:
            qs = q
            kc = kc_ref[0, heads, :].astype(BF16)
            kn = kn_ref[0, :, lanes].astype(BF16)
            cn = cd_ref[0, h:h + 1, :]
            cref = cn[:, 0:1]
            bias_c = LOG2E * (cref - cp_ref[0, h:h + 1, :])
            bias_n = tile + LOG2E * (cref - cn)
        if mode == "diff":
            s_c = jnp.dot(qs, kc_t, preferred_element_type=F32) + bias_c
        else:
            s_c = lax.dot_general(qs, kc, _NT, preferred_element_type=F32) + bias_c
        s_n = lax.dot_general(qs, kn, _NT, preferred_element_type=F32) + bias_n
        m = jnp.maximum(jnp.max(s_c, axis=1, keepdims=True), jnp.max(s_n, axis=1, keepdims=True))
        p_c = jnp.exp2(s_c - m)
        p_n = jnp.exp2(s_n - m)
        l = jnp.sum(p_c, axis=1, keepdims=True) + jnp.sum(p_n, axis=1, keepdims=True)
        acc = (jnp.dot(p_c.astype(BF16), vc, preferred_element_type=F32)
               + jnp.dot(p_n.astype(BF16), vn, preferred_element_type=F32))
        o = acc / l
        if mode == "diff":
            o = o[:tq] - _lambda(lamv_ref, lam_init) * o[tq:]
            o = o * _rms_scale(o) * subg_ref[...] * (1.0 - lam_init)
        o_ref[0, :, lanes] = (_gate(g_ref[0, :, lanes]) * o).astype(o_ref.dtype)


def _cache_attn_call(mode, q, kc, vc, kn, vn, g, tile, extra, *, nh, lp, lam_init=0.0):
    b, tq, w = q.shape
    row3 = lambda bi: (bi, 0, 0)
    const2 = lambda bi: (0, 0)
    new_spec = pl.BlockSpec((1, tq, w), row3)
    whole = lambda a: pl.BlockSpec((1,) + a.shape[1:], lambda bi: (bi,) + (0,) * (a.ndim - 1))
    in_specs = [new_spec, whole(kc), whole(vc), whole(kn), new_spec, new_spec,
                pl.BlockSpec((tq, tq), const2)]
    args = [q, kc, vc, kn, vn, g, tile]
    if mode == "diff":
        slopes, lamv, subg = extra
        in_specs += [pl.BlockSpec(memory_space=pltpu.SMEM),
                     pl.BlockSpec((4, LANES), const2),
                     pl.BlockSpec((1, HEAD), const2)]
        args += [slopes, lamv, subg]
    else:
        cp, cd = extra
        in_specs += [pl.BlockSpec((1,) + cp.shape[1:], row3),
                     pl.BlockSpec((1,) + cd.shape[1:], row3)]
        args += [cp, cd]
    return pl.pallas_call(
        functools.partial(_cache_attn_kernel, mode=mode, nh=nh, lp=lp, lam_init=lam_init),
        grid=(b,),
        in_specs=in_specs,
        out_specs=new_spec,
        out_shape=jax.ShapeDtypeStruct((b, tq, w), BF16),
        compiler_params=_cparams(1),
        name=f"attn_{mode}_cache",
    )(*args)


def _out_kernel(x_ref, ma_ref, mb_ref, wa_ref, wb_ref, gf_ref, o_ref):
    hres = (x_ref[...]
            + jnp.dot(ma_ref[...], wa_ref[...], preferred_element_type=F32)
            + jnp.dot(mb_ref[...], wb_ref[...], preferred_element_type=F32))
    o_ref[...] = hres * _rms_scale(hres) * gf_ref[...]


def _out_call(x2, ma, mb, wa, wb, gf, tm):
    m, d = x2.shape
    da, db = ma.shape[1], mb.shape[1]
    return pl.pallas_call(
        _out_kernel,
        grid=(m // tm,),
        in_specs=[pl.BlockSpec((tm, d), lambda i: (i, 0)),
                  pl.BlockSpec((tm, da), lambda i: (i, 0)),
                  pl.BlockSpec((tm, db), lambda i: (i, 0)),
                  pl.BlockSpec((da, d), lambda i: (0, 0)),
                  pl.BlockSpec((db, d), lambda i: (0, 0)),
                  pl.BlockSpec((1, d), lambda i: (0, 0))],
        out_specs=pl.BlockSpec((tm, d), lambda i: (i, 0)),
        out_shape=jax.ShapeDtypeStruct((m, d), F32),
        compiler_params=_cparams(1),
        name="out_proj",
    )(x2, ma, mb, wa, wb, gf)


def _chunk_allowed(q, k):
    return (k // CHUNK) <= (q // CHUNK)


def _diff_tile_kq(t):
    k = jnp.arange(t)[:, None]
    q = jnp.arange(t)[None, :]
    return jnp.where(_chunk_allowed(q, k), jnp.minimum(0, 2 * (q - k)).astype(F32), NEG)


def _causal_tile_kq(t):
    k = jnp.arange(t)[:, None]
    q = jnp.arange(t)[None, :]
    return jnp.where(k <= q, 0.0, NEG).astype(F32)


def _diff_tile_qk(t):
    q = jnp.arange(t)[:, None]
    k = jnp.arange(t)[None, :]
    return jnp.where(_chunk_allowed(q, k), (q - jnp.abs(q - k)).astype(F32), NEG)


def _causal_tile_qk(t):
    return _causal_tile_kq(t).T


def _layer(x, past, params, *, tm, tq, cum_ch):
    (g_norm, w_bf, wf_hl, b_pad, slopes, lamv, subg, lam_init,
     wo_a, wo_b, g_final, d_a, nh_b) = params
    b, s, d = x.shape
    nh_a = d_a // HEAD
    x2 = x.reshape(b * s, d)
    q_scale_a = float((HEAD // 2) ** -0.5 * LOG2E)
    q_scale_b = float(HEAD ** -0.5 * LOG2E)

    prompt = past is None
    assert not prompt or tm == tq
    outs_a = _proj_call(x2, g_norm, w_bf, 0, d_a, q_scale_a, (2 * nh_a, HEAD // 2), tm, prompt)
    outs_b = _proj_call(x2, g_norm, w_bf, 1, d_a, q_scale_b, (d_a,), tm, prompt,
                        logf_args=(wf_hl, b_pad, nh_b))
    qa, ka, va, ga = outs_a[:4]
    qb, kb, vb, gb = outs_b[:4]
    logf = outs_b[-1]

    r3 = lambda a: a.reshape(b, s, a.shape[-1])
    if prompt:
        c = _cumsum_call(logf, s, cum_ch).reshape(b, s, nh_b)
        ma = _self_attn_call("diff", r3(qa), r3(outs_a[4]), outs_a[5], r3(ga), _diff_tile_kq(tq),
                             (slopes, lamv, subg), tq=tq, hp=2, lam_init=lam_init)
        mb = _self_attn_call("fox", r3(qb), r3(outs_b[4]), outs_b[5], r3(gb), _causal_tile_kq(tq),
                             c, tq=tq, hp=2)
    else:
        cak, cav, cbk, cbv, cbf = past
        lp = cak.shape[1]
        assert lp % CHUNK == 0 and s == tq <= CHUNK
        tot = lp + s
        padded = -(-tot // cum_ch) * cum_ch
        lf_all = jnp.concatenate([cbf, r3(logf), jnp.zeros((b, padded - tot, nh_b), F32)], axis=1)
        c = _cumsum_call(lf_all.reshape(b * padded, nh_b), padded, cum_ch)
        c = jnp.transpose(c.reshape(b, padded, nh_b), (0, 2, 1))
        slab = lambda a: a.reshape(b, lp * a.shape[2], HEAD)
        cak_t = jnp.transpose(cak, (0, 2, 3, 1))
        ma = _cache_attn_call("diff", r3(qa), cak_t, slab(cav), ka.reshape((b, s) + ka.shape[1:]),
                              r3(va), r3(ga), _diff_tile_qk(tq), (slopes, lamv, subg),
                              nh=nh_a, lp=lp, lam_init=lam_init)
        mb = _cache_attn_call("fox", r3(qb), slab(cbk), slab(cbv), r3(kb), r3(vb), r3(gb),
                              _causal_tile_qk(tq), (c[:, :, :lp], c[:, :, lp:tot]),
                              nh=nh_b, lp=lp)

    y = _out_call(x2, ma.reshape(b * s, -1), mb.reshape(b * s, -1), wo_a, wo_b, g_final,
                  min(tm, 512))
    return y.reshape(b, s, d), ka, va, kb, vb, logf


def kernel(x_prompt, x_sample, cache_a_k, cache_a_v, cache_b_k, cache_b_v, cache_b_logf,
           g_norm, w_in, b_f, lam_q1, lam_k1, lam_q2, lam_k2, sub_gain, w_out, g_final):
    depth, d = g_norm.shape
    assert depth == 1
    nh_a2, hqk = cache_a_k.shape[3], cache_a_k.shape[4]
    nh_a = nh_a2 // 2
    nh_b = cache_b_k.shape[3]
    d_a = nh_a2 * hqk
    assert 2 * hqk == HEAD and cache_b_k.shape[4] == HEAD and d == 2 * d_a
    lyr = 0
    lam_init = 0.8 - 0.6 * math.exp(-0.3 * lyr)

    w = jnp.transpose(w_in[lyr])
    w_bf = w.astype(BF16)
    wf = jnp.pad(w[8 * d_a:], ((0, LANES - nh_b), (0, 0)))
    wf_hi = wf.astype(BF16)
    wf_hl = jnp.concatenate([wf_hi, (wf - wf_hi.astype(F32)).astype(BF16)], axis=0)
    b_pad = jnp.pad(b_f[lyr].astype(F32), (0, LANES - nh_b)).reshape(1, LANES)
    slopes = 2.0 ** (-8.0 * jnp.arange(1, nh_a + 1, dtype=F32) / nh_a)
    lamv = jnp.pad(jnp.stack([lam_q1[lyr], lam_k1[lyr], lam_q2[lyr], lam_k2[lyr]]).astype(F32),
                   ((0, 0), (0, LANES - hqk)))
    subg = sub_gain[lyr].astype(F32).reshape(1, HEAD)
    wo = w_out[lyr].astype(BF16)
    params = (g_norm[lyr].reshape(1, d), w_bf, wf_hl, b_pad, slopes, lamv, subg, lam_init,
              wo[:d_a], wo[d_a:], g_final.reshape(1, d), d_a, nh_b)

    bp, sp, _ = x_prompt.shape
    bs, ss, _ = x_sample.shape
    yp, pak, pav, pbk, pbv, pbf = _layer(x_prompt, None, params, tm=512, tq=512, cum_ch=256)
    past = (cache_a_k[lyr], cache_a_v[lyr], cache_b_k[lyr], cache_b_v[lyr], cache_b_logf[lyr])
    ys, sak, sav, sbk, sbv, sbf = _layer(x_sample, past, params, tm=bs * ss, tq=ss, cum_ch=256)

    shp = lambda a, bb, s_, n, e: a.reshape(1, bb, s_, n, e)
    return (yp, ys,
            shp(pak, bp, sp, nh_a2, hqk), shp(pav, bp, sp, nh_a, HEAD),
            shp(pbk, bp, sp, nh_b, HEAD), shp(pbv, bp, sp, nh_b, HEAD),
            pbf.reshape(1, bp, sp, nh_b),
            shp(sak, bs, ss, nh_a2, hqk), shp(sav, bs, ss, nh_a, HEAD),
            shp(sbk, bs, ss, nh_b, HEAD), shp(sbv, bs, ss, nh_b, HEAD),
            sbf.reshape(1, bs, ss, nh_b))
```

```python
import functools
import math

import jax
import jax.numpy as jnp
from jax import lax
from jax.experimental import pallas as pl
from jax.experimental.pallas import tpu as pltpu

F32 = jnp.float32
BF16 = jnp.bfloat16

EPS = 1e-6
CHUNK = 64
HEAD = 128
NEG = -1e30
LANES = 128
LOG2E = math.log2(math.e)
N_BIAS = 3
QCOLS = 256
VMEM_LIMIT = 56 * 1024 * 1024

_NT = (((1,), (1,)), ((), ()))


def _cparams(n_axes):
    return pltpu.CompilerParams(
        dimension_semantics=("arbitrary",) * n_axes, vmem_limit_bytes=VMEM_LIMIT)


def _rms_scale(x):
    return lax.rsqrt(jnp.mean(x * x, axis=-1, keepdims=True) + EPS)


def _split3(x):
    x1 = x.astype(BF16)
    r1 = x - x1.astype(F32)
    x2 = r1.astype(BF16)
    x3 = (r1 - x2.astype(F32)).astype(BF16)
    return x1, x2, x3


def _proj_kernel(*refs, q_scale, width, k_minor, attn_copies, nh_logf):
    x_ref, g_ref, w_ref = refs[:3]
    n_out = 4 + (2 if attn_copies else 0) + (1 if nh_logf else 0)
    o_refs = refs[-n_out:]
    q_ref, k_ref, v_ref, gate_ref = o_refs[:4]
    x = x_ref[...]
    xn32 = x * _rms_scale(x) * g_ref[...]
    xn = xn32.astype(BF16)
    slab = lambda j: lax.dot_general(xn, w_ref[j * width:(j + 1) * width, :], _NT,
                                     preferred_element_type=F32)
    q_ref[...] = (slab(0) * q_scale).astype(q_ref.dtype)
    k = slab(1)
    k_ref[...] = k.reshape(k_ref.shape)
    v = slab(2)
    v_ref[...] = v
    gate_ref[...] = slab(3).astype(gate_ref.dtype)
    if attn_copies:
        kbf_ref, vt_ref = o_refs[4:6]
        kbf_ref[...] = k.astype(BF16)
        for h in range(width // HEAD):
            vt_ref[h, 0] = v[:, h * HEAD:(h + 1) * HEAD].T.astype(BF16)
    if nh_logf:
        whl_ref, b_ref = refs[3:5]
        lo = (xn32 - xn.astype(F32)).astype(BF16)
        both = lax.dot_general(xn, whl_ref[...], _NT, preferred_element_type=F32)
        fl = (both[:, :LANES] + both[:, LANES:]
              + lax.dot_general(lo, whl_ref[:LANES, :], _NT, preferred_element_type=F32))
        t = fl + b_ref[...]
        logf = jnp.minimum(t, 0.0) - jnp.log(1.0 + jnp.exp(-jnp.abs(t)))
        o_refs[-1][...] = logf[:, :nh_logf]


def _proj_call(x2, g, w_bf, group, width, q_scale, k_minor, tm, attn_copies, logf_args=None):
    m, d = x2.shape
    const = lambda i: (0, 0)
    row = lambda i: (i, 0)
    krow = lambda i: (i,) + (0,) * len(k_minor)
    in_specs = [pl.BlockSpec((tm, d), row),
                pl.BlockSpec((1, d), const),
                pl.BlockSpec((4 * width, d), lambda i: (group, 0), pipeline_mode=pl.Buffered(1))]
    args = [x2, g, w_bf]
    flat = pl.BlockSpec((tm, width), row)
    out_specs = [flat, pl.BlockSpec((tm,) + k_minor, krow), flat, flat]
    out_shape = [jax.ShapeDtypeStruct((m, width), BF16),
                 jax.ShapeDtypeStruct((m,) + k_minor, F32),
                 jax.ShapeDtypeStruct((m, width), F32),
                 jax.ShapeDtypeStruct((m, width), BF16)]
    if attn_copies:
        nh = width // HEAD
        out_specs += [flat, pl.BlockSpec((nh, 1, HEAD, tm), lambda i: (0, i, 0, 0))]
        out_shape += [jax.ShapeDtypeStruct((m, width), BF16),
                      jax.ShapeDtypeStruct((nh, m // tm, HEAD, tm), BF16)]
    nh_logf = 0
    if logf_args is not None:
        wf_hl, b_pad, nh_logf = logf_args
        in_specs += [pl.BlockSpec((2 * LANES, d), const), pl.BlockSpec((1, LANES), const)]
        args += [wf_hl, b_pad]
        out_specs.append(pl.BlockSpec((tm, nh_logf), row))
        out_shape.append(jax.ShapeDtypeStruct((m, nh_logf), F32))
    return pl.pallas_call(
        functools.partial(_proj_kernel, q_scale=q_scale, width=width, k_minor=k_minor,
                          attn_copies=attn_copies, nh_logf=nh_logf),
        grid=(m // tm,),
        in_specs=in_specs,
        out_specs=out_specs,
        out_shape=out_shape,
        compiler_params=_cparams(1),
        name=f"proj_group{group}",
    )(*args)


def _cumsum_kernel(x_ref, l_ref, o_ref, *, ch):
    ltri = l_ref[...]
    carry = jnp.zeros((1, x_ref.shape[1]), F32)
    for c in range(x_ref.shape[0] // ch):
        x1, x2, x3 = _split3(x_ref[c * ch:(c + 1) * ch, :])
        y = (jnp.dot(ltri, x1, preferred_element_type=F32)
             + jnp.dot(ltri, x2, preferred_element_type=F32)
             + jnp.dot(ltri, x3, preferred_element_type=F32)) + carry
        o_ref[c * ch:(c + 1) * ch, :] = y
        carry = y[ch - 1:ch, :]


def _cumsum_call(x, ch):
    b, s, nh = x.shape
    cols = jnp.transpose(x, (1, 0, 2)).reshape(s, b * nh)
    tri = (jnp.arange(ch)[:, None] >= jnp.arange(ch)[None, :]).astype(BF16)
    out = pl.pallas_call(
        functools.partial(_cumsum_kernel, ch=ch),
        grid=(1,),
        in_specs=[pl.BlockSpec((s, b * nh), lambda i: (0, 0)),
                  pl.BlockSpec((ch, ch), lambda i: (0, 0))],
        out_specs=pl.BlockSpec((s, b * nh), lambda i: (0, 0)),
        out_shape=jax.ShapeDtypeStruct((s, b * nh), F32),
        compiler_params=_cparams(1),
        name="cumsum",
    )(cols, tri)
    return jnp.transpose(out.reshape(s, b, nh), (1, 0, 2))


def _lambda(lamv_ref, lam_init):
    lv = lamv_ref[...]
    return (jnp.exp(jnp.sum(lv[0:1] * lv[1:2], axis=1, keepdims=True))
            - jnp.exp(jnp.sum(lv[2:3] * lv[3:4], axis=1, keepdims=True)) + lam_init)


def _gate(g):
    g = g.astype(F32)
    return g / (1.0 + jnp.exp(-g))


def _stack_maps(q):
    lane = lax.broadcasted_iota(jnp.int32, q.shape, 1)
    zero = jnp.zeros_like(q)
    return jnp.concatenate([jnp.where(lane < HEAD // 2, q, zero),
                            jnp.where(lane >= HEAD // 2, q, zero)], axis=0)


def _self_attn_kernel(*refs, mode, tq, hp, lam_init):
    it = iter(refs)
    q_ref, k_ref, vt_ref, g_ref, tile_ref = next(it), next(it), next(it), next(it), next(it)
    if mode == "diff":
        slope_ref, lamv_ref, subg_ref = next(it), next(it), next(it)
    else:
        c_ref = next(it)
    o_ref = next(it)
    kaug_ref, qaug_ref, m_ref, l_ref, acc_ref, sa_ref, sb_ref = (next(it) for _ in range(7))

    hg = pl.program_id(1)
    nq = vt_ref.shape[1]
    tk = tq
    n = qaug_ref.shape[1]
    nhead = n // hp
    npairs = nq // 2
    if mode == "diff":
        bscales = [slope_ref[hg * hp + hs] * LOG2E for hs in range(hp)]

    lane = lax.broadcasted_iota(jnp.int32, (tk, HEAD), 1)
    qlane = lax.broadcasted_iota(jnp.int32, (nhead, HEAD), 1)
    for hs in range(hp):
        if mode == "diff":
            s1, s2, s3 = (piece.astype(F32)
                          for piece in _split3(jnp.full((1, HEAD), bscales[hs], F32)))
            ones = jnp.where(qlane < 2, s1, jnp.where(qlane < 4, s2,
                                                      jnp.where(qlane < 6, s3, 0.0))).astype(BF16)
        else:
            ones = jnp.where(qlane < N_BIAS, 1.0, 0.0).astype(BF16)
        hl = slice(hs * HEAD, (hs + 1) * HEAD)
        for j in range(nq):
            rows = slice(j * tk, (j + 1) * tk)
            if mode == "diff":
                pos = lax.broadcasted_iota(jnp.int32, (tk, HEAD), 0) + j * tk
                piece = jnp.where(lane % 2 == 0, (pos // 16) * 16, pos % 16)
                extra = jnp.where(lane < 6, piece, 0).astype(F32)
            else:
                c8 = c_ref[0, rows, :]
                sel = lax.broadcasted_iota(jnp.int32, c8.shape, 1) == hg * hp + hs
                bias = -LOG2E * jnp.sum(jnp.where(sel, c8, 0.0), axis=1, keepdims=True)
                b1, b2, b3 = (piece.astype(F32) for piece in _split3(bias))
                extra = jnp.where(lane == 0, b1, jnp.where(lane == 1, b2,
                                                           jnp.where(lane == 2, b3, 0.0)))
            kaug_ref[hs, rows, :HEAD] = k_ref[0, rows, hl]
            kaug_ref[hs, rows, HEAD:] = extra.astype(BF16)
            q = q_ref[0, rows, hl]
            qs = _stack_maps(q) if mode == "diff" else q
            qaug_ref[j, hs * nhead:(hs + 1) * nhead, :] = jnp.concatenate([qs, ones], axis=1)
    m_ref[...] = jnp.full(m_ref.shape, NEG, F32)
    l_ref[...] = jnp.zeros(l_ref.shape, F32)
    acc_ref[...] = jnp.zeros(acc_ref.shape, F32)

    groups = [(c0 // nhead, slice(c0, c0 + QCOLS)) for c0 in range(0, n, QCOLS)]

    tile = tile_ref[...]
    tiles = [bscales[hs] * tile if mode == "diff" else tile for hs in range(hp)]

    def product(p, s):
        if isinstance(s, int) and s == nq - 1:
            return p, p
        if isinstance(s, int) and s == nq:
            return nq - 1 - p, nq - 1 - p
        first = s < p
        return jnp.where(first, p, nq - 1 - p), jnp.where(first, s, s - p)

    def is_diag(s):
        return isinstance(s, int) and s >= nq - 1

    def nkeys(s, group):
        low_queries = (group[1].start % nhead) % tq + QCOLS <= tq // 2
        return tk // 2 if is_diag(s) and low_queries else tk

    def scores(p, s, group):
        hs, cols = group
        qb, j = product(p, s)
        nk = nkeys(s, group)
        k0 = j * tk if isinstance(j, int) else pl.multiple_of(j * tk, tk)
        sc = lax.dot_general(kaug_ref[hs, pl.ds(k0, nk), :], qaug_ref[qb, cols, :], _NT,
                             preferred_element_type=F32)
        if is_diag(s):
            t0 = (cols.start % nhead) % tq
            sc = sc + tiles[hs][:nk, t0:t0 + QCOLS]
        return sc

    def consume(s_ref, p, s, group):
        hs, cols = group
        qb, j = product(p, s)
        nk = nkeys(s, group)
        sc = s_ref[:nk, cols]
        m_prev = m_ref[qb, :, cols]
        m_new = jnp.maximum(m_prev, jnp.max(sc, axis=0, keepdims=True))
        alpha = jnp.exp2(m_prev - m_new)
        pr = jnp.exp2(sc - m_new)
        l_ref[qb, :, cols] = alpha * l_ref[qb, :, cols] + jnp.sum(pr, axis=0, keepdims=True)
        acc_ref[qb, :, cols] = alpha * acc_ref[qb, :, cols] + jnp.dot(
            vt_ref[hs, j, :, :nk], pr.astype(BF16), preferred_element_type=F32)
        m_ref[qb, :, cols] = m_new

    def pipe_step(nxt, cur, nxt_ref, cur_ref):
        for group in groups:
            sc = scores(*nxt, group)
            nxt_ref[:sc.shape[0], group[1]] = sc
            if cur is not None:
                consume(cur_ref, *cur, group)

    def run_pair(p, a_ref, b_ref):
        def two_slots(i, carry):
            pipe_step((p, 2 * i + 1), (p, 2 * i), b_ref, a_ref)
            pipe_step((p, 2 * i + 2), (p, 2 * i + 1), a_ref, b_ref)
            return carry

        lax.fori_loop(0, (nq - 2) // 2, two_slots, 0)
        pipe_step((p, nq - 1), (p, nq - 2), b_ref, a_ref)
        pipe_step((p, nq), (p, nq - 1), a_ref, b_ref)
        pipe_step((jnp.minimum(p + 1, npairs - 1), 0), (p, nq), b_ref, a_ref)

    pipe_step((0, 0), None, sa_ref, None)

    def two_pairs(i, carry):
        run_pair(2 * i, sa_ref, sb_ref)
        run_pair(2 * i + 1, sb_ref, sa_ref)
        return carry

    lax.fori_loop(0, npairs // 2, two_pairs, 0)

    def finalize_block(blk):
        rows = pl.ds(pl.multiple_of(blk * tq, tq), tq)
        for hs in range(hp):
            hl = slice(hs * HEAD, (hs + 1) * HEAD)
            hc = slice(hs * nhead, (hs + 1) * nhead)
            ot = acc_ref[blk, :, hc] * (1.0 / l_ref[blk, :, hc])
            if mode == "diff":
                ot = ot[:, :tq] - _lambda(lamv_ref, lam_init) * ot[:, tq:]
            o = ot.T
            if mode == "diff":
                o = o * _rms_scale(o) * subg_ref[...] * (1.0 - lam_init)
            o_ref[0, rows, hl] = (_gate(g_ref[0, rows, hl]) * o).astype(o_ref.dtype)

    def finalize_two(i, carry):
        finalize_block(2 * i)
        finalize_block(2 * i + 1)
        return carry

    lax.fori_loop(0, nq // 2, finalize_two, 0)


def _self_attn_call(mode, q, k, vt, g, tile, extra, *, tq, hp, lam_init=0.0):
    b, t, w = q.shape
    nh = w // HEAD
    nq = t // tq
    assert nq % 4 == 0 and nh % hp == 0
    assert vt.shape == (nh, b * nq, HEAD, tq)
    n = hp * (2 * tq if mode == "diff" else tq)
    seq_spec = pl.BlockSpec((1, t, hp * HEAD), lambda bi, hi: (bi, 0, hi))
    const2 = lambda bi, hi: (0, 0)
    in_specs = [seq_spec, seq_spec,
                pl.BlockSpec((hp, nq, HEAD, tq), lambda bi, hi: (hi, bi, 0, 0)),
                seq_spec,
                pl.BlockSpec((tq, tq), const2, pipeline_mode=pl.Buffered(1))]
    args = [q, k, vt, g, tile]
    if mode == "diff":
        slopes, lamv, subg = extra
        in_specs += [pl.BlockSpec(memory_space=pltpu.SMEM),
                     pl.BlockSpec((4, LANES), const2),
                     pl.BlockSpec((1, HEAD), const2)]
        args += [slopes, lamv, subg]
    else:
        c = extra
        in_specs += [pl.BlockSpec((1, t, c.shape[2]), lambda bi, hi: (bi, 0, 0),
                                  pipeline_mode=pl.Buffered(1))]
        args += [c]
    return pl.pallas_call(
        functools.partial(_self_attn_kernel, mode=mode, tq=tq, hp=hp, lam_init=lam_init),
        grid=(b, nh // hp),
        in_specs=in_specs,
        out_specs=seq_spec,
        out_shape=jax.ShapeDtypeStruct((b, t, w), BF16),
        scratch_shapes=[pltpu.VMEM((hp, t, 2 * HEAD), BF16),
                        pltpu.VMEM((nq, n, 2 * HEAD), BF16),
                        pltpu.VMEM((nq, 1, n), F32), pltpu.VMEM((nq, 1, n), F32),
                        pltpu.VMEM((nq, HEAD, n), F32),
                        pltpu.VMEM((tq, n), F32), pltpu.VMEM((tq, n), F32)],
        compiler_params=_cparams(2),
        name=f"attn_{mode}_self",
    )(*args)


def _cache_attn_kernel(*refs, mode, nh, lp, lam_init):
    it = iter(refs)
    q_ref, kc_ref, vc_ref, kn_ref, vn_ref, g_ref, tile_ref = (next(it) for _ in range(7))
    if mode == "diff":
        slope_ref, lamv_ref, subg_ref = next(it), next(it), next(it)
    else:
        cp_ref, cd_ref = next(it), next(it)
    o_ref = next(it)
    tq = q_ref.shape[1]
    tile = tile_ref[...]
    col = lax.broadcasted_iota(jnp.int32, (1, lp), 1).astype(F32)
    if mode == "diff":
        kn_all = kn_ref[0].reshape(tq, nh * HEAD).astype(BF16)

    for h in range(nh):
        lanes = slice(h * HEAD, (h + 1) * HEAD)
        heads = pl.ds(h, lp, stride=nh)
        q = q_ref[0, :, lanes]
        vn = vn_ref[0, :, lanes].astype(BF16)
        vc = vc_ref[0, heads, :].astype(BF16)
        if mode == "diff":
            qs = _stack_maps(q)
            kc_t = kc_ref[0, 2 * h:2 * h + 2].reshape(HEAD, lp).astype(BF16)
            kn = kn_all[:, lanes]
            bscale = slope_ref[h] * LOG2E
            bias_c = bscale * (col - float(lp))
            bias_n = bscale * jnp.concatenate([tile, tile], axis=0)
        else:
            qs = q
            kc = kc_ref[0, heads, :].astype(BF16)
            kn = kn_ref[0, :, lanes].astype(BF16)
            cn = cd_ref[0, h:h + 1, :]
            cref = cn[:, 0:1]
            bias_c = LOG2E * (cref - cp_ref[0, h:h + 1, :])
            bias_n = tile + LOG2E * (cref - cn)
        if mode == "diff":
            s_c = jnp.dot(qs, kc_t, preferred_element_type=F32) + bias_c
        else:
            s_c = lax.dot_general(qs, kc, _NT, preferred_element_type=F32) + bias_c
        s_n = lax.dot_general(qs, kn, _NT, preferred_element_type=F32) + bias_n
        m = jnp.maximum(jnp.max(s_c, axis=1, keepdims=True), jnp.max(s_n, axis=1, keepdims=True))
        p_c = jnp.exp2(s_c - m)
        p_n = jnp.exp2(s_n - m)
        l = jnp.sum(p_c, axis=1, keepdims=True) + jnp.sum(p_n, axis=1, keepdims=True)
        acc = (jnp.dot(p_c.astype(BF16), vc, preferred_element_type=F32)
               + jnp.dot(p_n.astype(BF16), vn, preferred_element_type=F32))
        o = acc / l
        if mode == "diff":
            o = o[:tq] - _lambda(lamv_ref, lam_init) * o[tq:]
            o = o * _rms_scale(o) * subg_ref[...] * (1.0 - lam_init)
        o_ref[0, :, lanes] = (_gate(g_ref[0, :, lanes]) * o).astype(o_ref.dtype)


def _cache_attn_call(mode, q, kc, vc, kn, vn, g, tile, extra, *, nh, lp, lam_init=0.0):
    b, tq, w = q.shape
    row3 = lambda bi: (bi, 0, 0)
    const2 = lambda bi: (0, 0)
    new_spec = pl.BlockSpec((1, tq, w), row3)
    whole = lambda a: pl.BlockSpec((1,) + a.shape[1:], lambda bi: (bi,) + (0,) * (a.ndim - 1))
    in_specs = [new_spec, whole(kc), whole(vc), whole(kn), new_spec, new_spec,
                pl.BlockSpec((tq, tq), const2)]
    args = [q, kc, vc, kn, vn, g, tile]
    if mode == "diff":
        slopes, lamv, subg = extra
        in_specs += [pl.BlockSpec(memory_space=pltpu.SMEM),
                     pl.BlockSpec((4, LANES), const2),
                     pl.BlockSpec((1, HEAD), const2)]
        args += [slopes, lamv, subg]
    else:
        cp, cd = extra
        in_specs += [pl.BlockSpec((1,) + cp.shape[1:], row3),
                     pl.BlockSpec((1,) + cd.shape[1:], row3)]
        args += [cp, cd]
    return pl.pallas_call(
        functools.partial(_cache_attn_kernel, mode=mode, nh=nh, lp=lp, lam_init=lam_init),
        grid=(b,),
        in_specs=in_specs,
        out_specs=new_spec,
        out_shape=jax.ShapeDtypeStruct((b, tq, w), BF16),
        compiler_params=_cparams(1),
        name=f"attn_{mode}_cache",
    )(*args)


def _out_kernel(x_ref, ma_ref, mb_ref, w_ref, gf_ref, o_ref, wbf_ref):
    @pl.when(pl.program_id(0) == 0)
    def _():
        wbf_ref[...] = w_ref[...].astype(BF16)

    da = ma_ref.shape[1]
    hres = (x_ref[...]
            + jnp.dot(ma_ref[...], wbf_ref[:da, :], preferred_element_type=F32)
            + jnp.dot(mb_ref[...], wbf_ref[da:, :], preferred_element_type=F32))
    o_ref[...] = hres * _rms_scale(hres) * gf_ref[...]


def _out_call(x2, ma, mb, w, gf, tm):
    m, d = x2.shape
    da, db = ma.shape[1], mb.shape[1]
    return pl.pallas_call(
        _out_kernel,
        grid=(m // tm,),
        in_specs=[pl.BlockSpec((tm, d), lambda i: (i, 0)),
                  pl.BlockSpec((tm, da), lambda i: (i, 0)),
                  pl.BlockSpec((tm, db), lambda i: (i, 0)),
                  pl.BlockSpec((da + db, d), lambda i: (0, 0), pipeline_mode=pl.Buffered(1)),
                  pl.BlockSpec((1, d), lambda i: (0, 0))],
        out_specs=pl.BlockSpec((tm, d), lambda i: (i, 0)),
        out_shape=jax.ShapeDtypeStruct((m, d), F32),
        scratch_shapes=[pltpu.VMEM((da + db, d), BF16)],
        compiler_params=_cparams(1),
        name="out_proj",
    )(x2, ma, mb, w, gf)


def _chunk_allowed(q, k):
    return (k // CHUNK) <= (q // CHUNK)


def _diff_tile_kq(t):
    k = jnp.arange(t)[:, None]
    q = jnp.arange(t)[None, :]
    return jnp.where(_chunk_allowed(q, k), jnp.minimum(0, 2 * (q - k)).astype(F32), NEG)


def _causal_tile_kq(t):
    k = jnp.arange(t)[:, None]
    q = jnp.arange(t)[None, :]
    return jnp.where(k <= q, 0.0, NEG).astype(F32)


def _diff_tile_qk(t):
    q = jnp.arange(t)[:, None]
    k = jnp.arange(t)[None, :]
    return jnp.where(_chunk_allowed(q, k), (q - jnp.abs(q - k)).astype(F32), NEG)


def _causal_tile_qk(t):
    return _causal_tile_kq(t).T


def _layer(x, past, params, *, tm, tq, cum_ch):
    (g_norm, w_bf, wf_hl, b_pad, slopes, lamv, subg, lam_init,
     w_out, g_final, d_a, nh_b) = params
    b, s, d = x.shape
    nh_a = d_a // HEAD
    x2 = x.reshape(b * s, d)
    q_scale_a = float((HEAD // 2) ** -0.5 * LOG2E)
    q_scale_b = float(HEAD ** -0.5 * LOG2E)

    prompt = past is None
    assert not prompt or tm == tq
    outs_a = _proj_call(x2, g_norm, w_bf, 0, d_a, q_scale_a, (2 * nh_a, HEAD // 2), tm, prompt)
    outs_b = _proj_call(x2, g_norm, w_bf, 1, d_a, q_scale_b, (d_a,), tm, prompt,
                        logf_args=(wf_hl, b_pad, nh_b))
    qa, ka, va, ga = outs_a[:4]
    qb, kb, vb, gb = outs_b[:4]
    logf = outs_b[-1]

    r3 = lambda a: a.reshape(b, s, a.shape[-1])
    if prompt:
        c = _cumsum_call(r3(logf), cum_ch)
        ma = _self_attn_call("diff", r3(qa), r3(outs_a[4]), outs_a[5], r3(ga), _diff_tile_kq(tq),
                             (slopes, lamv, subg), tq=tq, hp=2, lam_init=lam_init)
        mb = _self_attn_call("fox", r3(qb), r3(outs_b[4]), outs_b[5], r3(gb), _causal_tile_kq(tq),
                             c, tq=tq, hp=2)
    else:
        cak, cav, cbk, cbv, cbf = past
        lp = cak.shape[1]
        assert lp % CHUNK == 0 and s == tq <= CHUNK
        tot = lp + s
        padded = -(-tot // cum_ch) * cum_ch
        lf_all = jnp.concatenate([cbf, r3(logf), jnp.zeros((b, padded - tot, nh_b), F32)], axis=1)
        c = jnp.transpose(_cumsum_call(lf_all, cum_ch), (0, 2, 1))
        slab = lambda a: a.reshape(b, lp * a.shape[2], HEAD)
        cak_t = jnp.transpose(cak, (0, 2, 3, 1))
        ma = _cache_attn_call("diff", r3(qa), cak_t, slab(cav), ka.reshape((b, s) + ka.shape[1:]),
                              r3(va), r3(ga), _diff_tile_qk(tq), (slopes, lamv, subg),
                              nh=nh_a, lp=lp, lam_init=lam_init)
        mb = _cache_attn_call("fox", r3(qb), slab(cbk), slab(cbv), r3(kb), r3(vb), r3(gb),
                              _causal_tile_qk(tq), (c[:, :, :lp], c[:, :, lp:tot]),
                              nh=nh_b, lp=lp)

    y = _out_call(x2, ma.reshape(b * s, -1), mb.reshape(b * s, -1), w_out, g_final,
                  min(tm, 512))
    return y.reshape(b, s, d), ka, va, kb, vb, logf


def kernel(x_prompt, x_sample, cache_a_k, cache_a_v, cache_b_k, cache_b_v, cache_b_logf,
           g_norm, w_in, b_f, lam_q1, lam_k1, lam_q2, lam_k2, sub_gain, w_out, g_final):
    depth, d = g_norm.shape
    assert depth == 1
    nh_a2, hqk = cache_a_k.shape[3], cache_a_k.shape[4]
    nh_a = nh_a2 // 2
    nh_b = cache_b_k.shape[3]
    d_a = nh_a2 * hqk
    assert 2 * hqk == HEAD and cache_b_k.shape[4] == HEAD and d == 2 * d_a
    lyr = 0
    lam_init = 0.8 - 0.6 * math.exp(-0.3 * lyr)

    w = jnp.transpose(w_in[lyr])
    w_bf = w.astype(BF16)
    wf = jnp.pad(w[8 * d_a:], ((0, LANES - nh_b), (0, 0)))
    wf_hi = wf.astype(BF16)
    wf_hl = jnp.concatenate([wf_hi, (wf - wf_hi.astype(F32)).astype(BF16)], axis=0)
    b_pad = jnp.pad(b_f[lyr].astype(F32), (0, LANES - nh_b)).reshape(1, LANES)
    slopes = 2.0 ** (-8.0 * jnp.arange(1, nh_a + 1, dtype=F32) / nh_a)
    lamv = jnp.pad(jnp.stack([lam_q1[lyr], lam_k1[lyr], lam_q2[lyr], lam_k2[lyr]]).astype(F32),
                   ((0, 0), (0, LANES - hqk)))
    subg = sub_gain[lyr].astype(F32).reshape(1, HEAD)
    params = (g_norm[lyr].reshape(1, d), w_bf, wf_hl, b_pad, slopes, lamv, subg, lam_init,
              w_out[lyr], g_final.reshape(1, d), d_a, nh_b)

    bp, sp, _ = x_prompt.shape
    bs, ss, _ = x_sample.shape
    yp, pak, pav, pbk, pbv, pbf = _layer(x_prompt, None, params, tm=512, tq=512, cum_ch=256)
    past = (cache_a_k[lyr], cache_a_v[lyr], cache_b_k[lyr], cache_b_v[lyr], cache_b_logf[lyr])
    ys, sak, sav, sbk, sbv, sbf = _layer(x_sample, past, params, tm=bs * ss, tq=ss, cum_ch=256)

    shp = lambda a, bb, s_, n, e: a.reshape(1, bb, s_, n, e)
    return (yp, ys,
            shp(pak, bp, sp, nh_a2, hqk), shp(pav, bp, sp, nh_a, HEAD),
            shp(pbk, bp, sp, nh_b, HEAD), shp(pbv, bp, sp, nh_b, HEAD),
            pbf.reshape(1, bp, sp, nh_b),
            shp(sak, bs, ss, nh_a2, hqk), shp(sav, bs, ss, nh_a, HEAD),
            shp(sbk, bs, ss, nh_b, HEAD), shp(sbv, bs, ss, nh_b, HEAD),
            sbf.reshape(1, bs, ss, nh_b))
```

```python
import functools
import math

import jax
import jax.numpy as jnp
from jax import lax
from jax.experimental import pallas as pl
from jax.experimental.pallas import tpu as pltpu

F32 = jnp.float32
BF16 = jnp.bfloat16

EPS = 1e-6
CHUNK = 64
HEAD = 128
NEG = -1e30
LANES = 128
LOG2E = math.log2(math.e)
N_BIAS = 3
QCOLS = 256
VMEM_LIMIT = 56 * 1024 * 1024

_NT = (((1,), (1,)), ((), ()))


def _cparams(n_axes):
    return pltpu.CompilerParams(
        dimension_semantics=("arbitrary",) * n_axes, vmem_limit_bytes=VMEM_LIMIT)


def _rms_scale(x):
    return lax.rsqrt(jnp.mean(x * x, axis=-1, keepdims=True) + EPS)


def _split3(x):
    x1 = x.astype(BF16)
    r1 = x - x1.astype(F32)
    x2 = r1.astype(BF16)
    x3 = (r1 - x2.astype(F32)).astype(BF16)
    return x1, x2, x3


def _proj_kernel(*refs, q_scale, width, k_maps, attn_copies, nh_logf):
    x_ref, g_ref, w_ref = refs[:3]
    n_out = 4 + (2 if attn_copies else 0) + (1 if nh_logf else 0)
    o_refs = refs[-n_out:]
    q_ref, k_ref, v_ref, gate_ref = o_refs[:4]
    x = x_ref[...]
    xn32 = x * _rms_scale(x) * g_ref[...]
    xn = xn32.astype(BF16)
    slab = lambda j: lax.dot_general(xn, w_ref[j * width:(j + 1) * width, :], _NT,
                                     preferred_element_type=F32)
    nh = width // HEAD

    def store_heads(o_ref, z):
        for h in range(nh):
            o_ref[pl.ds(h, z.shape[0], stride=nh), :] = z[:, h * HEAD:(h + 1) * HEAD]

    q_ref[...] = (slab(0) * q_scale).astype(q_ref.dtype)
    k = slab(1)
    if k_maps:
        k_ref[...] = k.reshape(k_ref.shape)
    else:
        store_heads(k_ref, k)
    v = slab(2)
    store_heads(v_ref, v)
    gate_ref[...] = slab(3).astype(gate_ref.dtype)
    if attn_copies:
        kbf_ref, vt_ref = o_refs[4:6]
        kbf_ref[...] = k.astype(BF16)
        for h in range(nh):
            vt_ref[h, 0] = v[:, h * HEAD:(h + 1) * HEAD].T.astype(BF16)
    if nh_logf:
        whl_ref, b_ref = refs[3:5]
        lo = (xn32 - xn.astype(F32)).astype(BF16)
        both = lax.dot_general(xn, whl_ref[...], _NT, preferred_element_type=F32)
        fl = (both[:, :LANES] + both[:, LANES:]
              + lax.dot_general(lo, whl_ref[:LANES, :], _NT, preferred_element_type=F32))
        t = fl + b_ref[...]
        logf = jnp.minimum(t, 0.0) - jnp.log(1.0 + jnp.exp(-jnp.abs(t)))
        o_refs[-1][...] = logf[:, :nh_logf]


def _proj_call(x2, g, w_bf, group, width, q_scale, k_maps, tm, attn_copies, logf_args=None):
    m, d = x2.shape
    const = lambda i: (0, 0)
    row = lambda i: (i, 0)
    nh = width // HEAD
    heads = pl.BlockSpec((tm * nh, HEAD), row)
    heads_shape = jax.ShapeDtypeStruct((m * nh, HEAD), F32)
    maps = pl.BlockSpec((tm, 2 * nh, HEAD // 2), lambda i: (i, 0, 0))
    maps_shape = jax.ShapeDtypeStruct((m, 2 * nh, HEAD // 2), F32)
    in_specs = [pl.BlockSpec((tm, d), row),
                pl.BlockSpec((1, d), const),
                pl.BlockSpec((4 * width, d), lambda i: (group, 0), pipeline_mode=pl.Buffered(1))]
    args = [x2, g, w_bf]
    flat = pl.BlockSpec((tm, width), row)
    out_specs = [flat, maps if k_maps else heads, heads, flat]
    out_shape = [jax.ShapeDtypeStruct((m, width), BF16),
                 maps_shape if k_maps else heads_shape,
                 heads_shape,
                 jax.ShapeDtypeStruct((m, width), BF16)]
    if attn_copies:
        out_specs += [flat, pl.BlockSpec((nh, 1, HEAD, tm), lambda i: (0, i, 0, 0))]
        out_shape += [jax.ShapeDtypeStruct((m, width), BF16),
                      jax.ShapeDtypeStruct((nh, m // tm, HEAD, tm), BF16)]
    nh_logf = 0
    if logf_args is not None:
        wf_hl, b_pad, nh_logf = logf_args
        in_specs += [pl.BlockSpec((2 * LANES, d), const), pl.BlockSpec((1, LANES), const)]
        args += [wf_hl, b_pad]
        out_specs.append(pl.BlockSpec((tm, nh_logf), row))
        out_shape.append(jax.ShapeDtypeStruct((m, nh_logf), F32))
    return pl.pallas_call(
        functools.partial(_proj_kernel, q_scale=q_scale, width=width, k_maps=k_maps,
                          attn_copies=attn_copies, nh_logf=nh_logf),
        grid=(m // tm,),
        in_specs=in_specs,
        out_specs=out_specs,
        out_shape=out_shape,
        compiler_params=_cparams(1),
        name=f"proj_group{group}",
    )(*args)


def _cumsum_kernel(x_ref, l_ref, o_ref, *, ch):
    ltri = l_ref[...]
    carry = jnp.zeros((1, x_ref.shape[1]), F32)
    for c in range(x_ref.shape[0] // ch):
        x1, x2, x3 = _split3(x_ref[c * ch:(c + 1) * ch, :])
        y = (jnp.dot(ltri, x1, preferred_element_type=F32)
             + jnp.dot(ltri, x2, preferred_element_type=F32)
             + jnp.dot(ltri, x3, preferred_element_type=F32)) + carry
        o_ref[c * ch:(c + 1) * ch, :] = y
        carry = y[ch - 1:ch, :]


def _cumsum_call(x, ch):
    b, s, nh = x.shape
    cols = jnp.transpose(x, (1, 0, 2)).reshape(s, b * nh)
    tri = (jnp.arange(ch)[:, None] >= jnp.arange(ch)[None, :]).astype(BF16)
    out = pl.pallas_call(
        functools.partial(_cumsum_kernel, ch=ch),
        grid=(1,),
        in_specs=[pl.BlockSpec((s, b * nh), lambda i: (0, 0)),
                  pl.BlockSpec((ch, ch), lambda i: (0, 0))],
        out_specs=pl.BlockSpec((s, b * nh), lambda i: (0, 0)),
        out_shape=jax.ShapeDtypeStruct((s, b * nh), F32),
        compiler_params=_cparams(1),
        name="cumsum",
    )(cols, tri)
    return jnp.transpose(out.reshape(s, b, nh), (1, 0, 2))


def _lambda(lamv_ref, lam_init):
    lv = lamv_ref[...]
    return (jnp.exp(jnp.sum(lv[0:1] * lv[1:2], axis=1, keepdims=True))
            - jnp.exp(jnp.sum(lv[2:3] * lv[3:4], axis=1, keepdims=True)) + lam_init)


def _gate(g):
    g = g.astype(F32)
    return g / (1.0 + jnp.exp(-g))


def _stack_maps(q):
    lane = lax.broadcasted_iota(jnp.int32, q.shape, 1)
    zero = jnp.zeros_like(q)
    return jnp.concatenate([jnp.where(lane < HEAD // 2, q, zero),
                            jnp.where(lane >= HEAD // 2, q, zero)], axis=0)


def _self_attn_kernel(*refs, mode, tq, hp, lam_init):
    it = iter(refs)
    q_ref, k_ref, vt_ref, g_ref, tile_ref = next(it), next(it), next(it), next(it), next(it)
    if mode == "diff":
        slope_ref, lamv_ref, subg_ref = next(it), next(it), next(it)
    else:
        c_ref = next(it)
    o_ref = next(it)
    kaug_ref, qaug_ref, m_ref, l_ref, acc_ref, sa_ref, sb_ref = (next(it) for _ in range(7))

    hg = pl.program_id(1)
    nq = vt_ref.shape[1]
    tk = tq
    n = qaug_ref.shape[1]
    nhead = n // hp
    npairs = nq // 2
    if mode == "diff":
        bscales = [slope_ref[hg * hp + hs] * LOG2E for hs in range(hp)]

    lane = lax.broadcasted_iota(jnp.int32, (tk, HEAD), 1)
    qlane = lax.broadcasted_iota(jnp.int32, (nhead, HEAD), 1)
    for hs in range(hp):
        if mode == "diff":
            s1, s2, s3 = (piece.astype(F32)
                          for piece in _split3(jnp.full((1, HEAD), bscales[hs], F32)))
            ones = jnp.where(qlane < 2, s1, jnp.where(qlane < 4, s2,
                                                      jnp.where(qlane < 6, s3, 0.0))).astype(BF16)
        else:
            ones = jnp.where(qlane < N_BIAS, 1.0, 0.0).astype(BF16)
        hl = slice(hs * HEAD, (hs + 1) * HEAD)
        for j in range(nq):
            rows = slice(j * tk, (j + 1) * tk)
            if mode == "diff":
                pos = lax.broadcasted_iota(jnp.int32, (tk, HEAD), 0) + j * tk
                piece = jnp.where(lane % 2 == 0, (pos // 16) * 16, pos % 16)
                extra = jnp.where(lane < 6, piece, 0).astype(F32)
            else:
                c8 = c_ref[0, rows, :]
                sel = lax.broadcasted_iota(jnp.int32, c8.shape, 1) == hg * hp + hs
                bias = -LOG2E * jnp.sum(jnp.where(sel, c8, 0.0), axis=1, keepdims=True)
                b1, b2, b3 = (piece.astype(F32) for piece in _split3(bias))
                extra = jnp.where(lane == 0, b1, jnp.where(lane == 1, b2,
                                                           jnp.where(lane == 2, b3, 0.0)))
            kaug_ref[hs, rows, :HEAD] = k_ref[0, rows, hl]
            kaug_ref[hs, rows, HEAD:] = extra.astype(BF16)
            q = q_ref[0, rows, hl]
            qs = _stack_maps(q) if mode == "diff" else q
            qaug_ref[j, hs * nhead:(hs + 1) * nhead, :] = jnp.concatenate([qs, ones], axis=1)
    m_ref[...] = jnp.full(m_ref.shape, NEG, F32)
    l_ref[...] = jnp.zeros(l_ref.shape, F32)
    acc_ref[...] = jnp.zeros(acc_ref.shape, F32)

    groups = [(c0 // nhead, slice(c0, c0 + QCOLS)) for c0 in range(0, n, QCOLS)]

    tile = tile_ref[...]
    tiles = [bscales[hs] * tile if mode == "diff" else tile for hs in range(hp)]

    def product(p, s):
        if isinstance(s, int) and s == nq - 1:
            return p, p
        if isinstance(s, int) and s == nq:
            return nq - 1 - p, nq - 1 - p
        first = s < p
        return jnp.where(first, p, nq - 1 - p), jnp.where(first, s, s - p)

    def is_diag(s):
        return isinstance(s, int) and s >= nq - 1

    def nkeys(s, group):
        low_queries = (group[1].start % nhead) % tq + QCOLS <= tq // 2
        return tk // 2 if is_diag(s) and low_queries else tk

    def scores(p, s, group):
        hs, cols = group
        qb, j = product(p, s)
        nk = nkeys(s, group)
        k0 = j * tk if isinstance(j, int) else pl.multiple_of(j * tk, tk)
        sc = lax.dot_general(kaug_ref[hs, pl.ds(k0, nk), :], qaug_ref[qb, cols, :], _NT,
                             preferred_element_type=F32)
        if is_diag(s):
            t0 = (cols.start % nhead) % tq
            sc = sc + tiles[hs][:nk, t0:t0 + QCOLS]
        return sc

    def consume(s_ref, p, s, group):
        hs, cols = group
        qb, j = product(p, s)
        nk = nkeys(s, group)
        sc = s_ref[:nk, cols]
        m_prev = m_ref[qb, :, cols]
        m_new = jnp.maximum(m_prev, jnp.max(sc, axis=0, keepdims=True))
        alpha = jnp.exp2(m_prev - m_new)
        pr = jnp.exp2(sc - m_new)
        l_ref[qb, :, cols] = alpha * l_ref[qb, :, cols] + jnp.sum(pr, axis=0, keepdims=True)
        acc_ref[qb, :, cols] = alpha * acc_ref[qb, :, cols] + jnp.dot(
            vt_ref[hs, j, :, :nk], pr.astype(BF16), preferred_element_type=F32)
        m_ref[qb, :, cols] = m_new

    def pipe_step(nxt, cur, nxt_ref, cur_ref):
        for gi, group in enumerate(groups):
            sc = scores(*nxt, group)
            nxt_ref[:sc.shape[0], group[1]] = sc
            if cur is not None and gi >= 1:
                consume(cur_ref, *cur, groups[gi - 1])
        if cur is not None:
            consume(cur_ref, *cur, groups[-1])

    def run_pair(p, a_ref, b_ref):
        def two_slots(i, carry):
            pipe_step((p, 2 * i + 1), (p, 2 * i), b_ref, a_ref)
            pipe_step((p, 2 * i + 2), (p, 2 * i + 1), a_ref, b_ref)
            return carry

        lax.fori_loop(0, (nq - 2) // 2, two_slots, 0)
        pipe_step((p, nq - 1), (p, nq - 2), b_ref, a_ref)
        pipe_step((p, nq), (p, nq - 1), a_ref, b_ref)
        pipe_step((jnp.minimum(p + 1, npairs - 1), 0), (p, nq), b_ref, a_ref)

    pipe_step((0, 0), None, sa_ref, None)

    def two_pairs(i, carry):
        run_pair(2 * i, sa_ref, sb_ref)
        run_pair(2 * i + 1, sb_ref, sa_ref)
        return carry

    lax.fori_loop(0, npairs // 2, two_pairs, 0)

    def finalize_block(blk):
        rows = pl.ds(pl.multiple_of(blk * tq, tq), tq)
        for hs in range(hp):
            hl = slice(hs * HEAD, (hs + 1) * HEAD)
            hc = slice(hs * nhead, (hs + 1) * nhead)
            ot = acc_ref[blk, :, hc] * (1.0 / l_ref[blk, :, hc])
            if mode == "diff":
                ot = ot[:, :tq] - _lambda(lamv_ref, lam_init) * ot[:, tq:]
            o = ot.T
            if mode == "diff":
                o = o * _rms_scale(o) * subg_ref[...] * (1.0 - lam_init)
            o_ref[0, rows, hl] = (_gate(g_ref[0, rows, hl]) * o).astype(o_ref.dtype)

    def finalize_two(i, carry):
        finalize_block(2 * i)
        finalize_block(2 * i + 1)
        return carry

    lax.fori_loop(0, nq // 2, finalize_two, 0)


def _self_attn_call(mode, q, k, vt, g, tile, extra, *, tq, hp, lam_init=0.0):
    b, t, w = q.shape
    nh = w // HEAD
    nq = t // tq
    assert nq % 4 == 0 and nh % hp == 0
    assert vt.shape == (nh, b * nq, HEAD, tq)
    n = hp * (2 * tq if mode == "diff" else tq)
    seq_spec = pl.BlockSpec((1, t, hp * HEAD), lambda bi, hi: (bi, 0, hi))
    const2 = lambda bi, hi: (0, 0)
    in_specs = [seq_spec, seq_spec,
                pl.BlockSpec((hp, nq, HEAD, tq), lambda bi, hi: (hi, bi, 0, 0)),
                seq_spec,
                pl.BlockSpec((tq, tq), const2, pipeline_mode=pl.Buffered(1))]
    args = [q, k, vt, g, tile]
    if mode == "diff":
        slopes, lamv, subg = extra
        in_specs += [pl.BlockSpec(memory_space=pltpu.SMEM),
                     pl.BlockSpec((4, LANES), const2),
                     pl.BlockSpec((1, HEAD), const2)]
        args += [slopes, lamv, subg]
    else:
        c = extra
        in_specs += [pl.BlockSpec((1, t, c.shape[2]), lambda bi, hi: (bi, 0, 0),
                                  pipeline_mode=pl.Buffered(1))]
        args += [c]
    return pl.pallas_call(
        functools.partial(_self_attn_kernel, mode=mode, tq=tq, hp=hp, lam_init=lam_init),
        grid=(b, nh // hp),
        in_specs=in_specs,
        out_specs=seq_spec,
        out_shape=jax.ShapeDtypeStruct((b, t, w), BF16),
        scratch_shapes=[pltpu.VMEM((hp, t, 2 * HEAD), BF16),
                        pltpu.VMEM((nq, n, 2 * HEAD), BF16),
                        pltpu.VMEM((nq, 1, n), F32), pltpu.VMEM((nq, 1, n), F32),
                        pltpu.VMEM((nq, HEAD, n), F32),
                        pltpu.VMEM((tq, n), F32), pltpu.VMEM((tq, n), F32)],
        compiler_params=_cparams(2),
        name=f"attn_{mode}_self",
    )(*args)


def _cache_attn_kernel(*refs, mode, nh, lp, lam_init):
    it = iter(refs)
    q_ref, kc_ref, vc_ref, kn_ref, vn_ref, g_ref, tile_ref = (next(it) for _ in range(7))
    if mode == "diff":
        slope_ref, lamv_ref, subg_ref = next(it), next(it), next(it)
    else:
        cp_ref, cd_ref = next(it), next(it)
    o_ref = next(it)
    tq = q_ref.shape[1]
    tile = tile_ref[...]
    col = lax.broadcasted_iota(jnp.int32, (1, lp), 1).astype(F32)
    if mode == "diff":
        kn_all = kn_ref[0].reshape(tq, nh * HEAD).astype(BF16)

    for h in range(nh):
        lanes = slice(h * HEAD, (h + 1) * HEAD)
        heads = pl.ds(h, lp, stride=nh)
        q = q_ref[0, :, lanes]
        new_heads = pl.ds(h, tq, stride=nh)
        vn = vn_ref[0, new_heads, :].astype(BF16)
        vc = vc_ref[0, heads, :].astype(BF16)
        if mode == "diff":
            qs = _stack_maps(q)
            kc_t = kc_ref[0, 2 * h:2 * h + 2].reshape(HEAD, lp).astype(BF16)
            kn = kn_all[:, lanes]
            bscale = slope_ref[h] * LOG2E
            bias_c = bscale * (col - float(lp))
            bias_n = bscale * jnp.concatenate([tile, tile], axis=0)
        else:
            qs = q
            kc = kc_ref[0, heads, :].astype(BF16)
            kn = kn_ref[0, new_heads, :].astype(BF16)
            cn = cd_ref[0, h:h + 1, :]
            cref = cn[:, 0:1]
            bias_c = LOG2E * (cref - cp_ref[0, h:h + 1, :])
            bias_n = tile + LOG2E * (cref - cn)
        if mode == "diff":
            s_c = jnp.dot(qs, kc_t, preferred_element_type=F32) + bias_c
        else:
            s_c = lax.dot_general(qs, kc, _NT, preferred_element_type=F32) + bias_c
        s_n = lax.dot_general(qs, kn, _NT, preferred_element_type=F32) + bias_n
        m = jnp.maximum(jnp.max(s_c, axis=1, keepdims=True), jnp.max(s_n, axis=1, keepdims=True))
        p_c = jnp.exp2(s_c - m)
        p_n = jnp.exp2(s_n - m)
        l = jnp.sum(p_c, axis=1, keepdims=True) + jnp.sum(p_n, axis=1, keepdims=True)
        acc = (jnp.dot(p_c.astype(BF16), vc, preferred_element_type=F32)
               + jnp.dot(p_n.astype(BF16), vn, preferred_element_type=F32))
        o = acc / l
        if mode == "diff":
            o = o[:tq] - _lambda(lamv_ref, lam_init) * o[tq:]
            o = o * _rms_scale(o) * subg_ref[...] * (1.0 - lam_init)
        o_ref[0, :, lanes] = (_gate(g_ref[0, :, lanes]) * o).astype(o_ref.dtype)


def _cache_attn_call(mode, q, kc, vc, kn, vn, g, tile, extra, *, nh, lp, lam_init=0.0):
    b, tq, w = q.shape
    row3 = lambda bi: (bi, 0, 0)
    const2 = lambda bi: (0, 0)
    new_spec = pl.BlockSpec((1, tq, w), row3)
    whole = lambda a: pl.BlockSpec((1,) + a.shape[1:], lambda bi: (bi,) + (0,) * (a.ndim - 1))
    in_specs = [new_spec, whole(kc), whole(vc), whole(kn), whole(vn), new_spec,
                pl.BlockSpec((tq, tq), const2)]
    args = [q, kc, vc, kn, vn, g, tile]
    if mode == "diff":
        slopes, lamv, subg = extra
        in_specs += [pl.BlockSpec(memory_space=pltpu.SMEM),
                     pl.BlockSpec((4, LANES), const2),
                     pl.BlockSpec((1, HEAD), const2)]
        args += [slopes, lamv, subg]
    else:
        cp, cd = extra
        in_specs += [pl.BlockSpec((1,) + cp.shape[1:], row3),
                     pl.BlockSpec((1,) + cd.shape[1:], row3)]
        args += [cp, cd]
    return pl.pallas_call(
        functools.partial(_cache_attn_kernel, mode=mode, nh=nh, lp=lp, lam_init=lam_init),
        grid=(b,),
        in_specs=in_specs,
        out_specs=new_spec,
        out_shape=jax.ShapeDtypeStruct((b, tq, w), BF16),
        compiler_params=_cparams(1),
        name=f"attn_{mode}_cache",
    )(*args)


def _out_kernel(x_ref, ma_ref, mb_ref, w_ref, gf_ref, o_ref, wbf_ref):
    @pl.when(pl.program_id(0) == 0)
    def _():
        wbf_ref[...] = w_ref[...].astype(BF16)

    da = ma_ref.shape[1]
    hres = (x_ref[...]
            + jnp.dot(ma_ref[...], wbf_ref[:da, :], preferred_element_type=F32)
            + jnp.dot(mb_ref[...], wbf_ref[da:, :], preferred_element_type=F32))
    o_ref[...] = hres * _rms_scale(hres) * gf_ref[...]


def _out_call(x2, ma, mb, w, gf, tm):
    m, d = x2.shape
    da, db = ma.shape[1], mb.shape[1]
    return pl.pallas_call(
        _out_kernel,
        grid=(m // tm,),
        in_specs=[pl.BlockSpec((tm, d), lambda i: (i, 0)),
                  pl.BlockSpec((tm, da), lambda i: (i, 0)),
                  pl.BlockSpec((tm, db), lambda i: (i, 0)),
                  pl.BlockSpec((da + db, d), lambda i: (0, 0), pipeline_mode=pl.Buffered(1)),
                  pl.BlockSpec((1, d), lambda i: (0, 0))],
        out_specs=pl.BlockSpec((tm, d), lambda i: (i, 0)),
        out_shape=jax.ShapeDtypeStruct((m, d), F32),
        scratch_shapes=[pltpu.VMEM((da + db, d), BF16)],
        compiler_params=_cparams(1),
        name="out_proj",
    )(x2, ma, mb, w, gf)


def _chunk_allowed(q, k):
    return (k // CHUNK) <= (q // CHUNK)


def _diff_tile_kq(t):
    k = jnp.arange(t)[:, None]
    q = jnp.arange(t)[None, :]
    return jnp.where(_chunk_allowed(q, k), jnp.minimum(0, 2 * (q - k)).astype(F32), NEG)


def _causal_tile_kq(t):
    k = jnp.arange(t)[:, None]
    q = jnp.arange(t)[None, :]
    return jnp.where(k <= q, 0.0, NEG).astype(F32)


def _diff_tile_qk(t):
    q = jnp.arange(t)[:, None]
    k = jnp.arange(t)[None, :]
    return jnp.where(_chunk_allowed(q, k), (q - jnp.abs(q - k)).astype(F32), NEG)


def _causal_tile_qk(t):
    return _causal_tile_kq(t).T


def _layer(x, past, params, *, tm, tq, cum_ch):
    (g_norm, w_bf, wf_hl, b_pad, slopes, lamv, subg, lam_init,
     w_out, g_final, d_a, nh_b) = params
    b, s, d = x.shape
    nh_a = d_a // HEAD
    x2 = x.reshape(b * s, d)
    q_scale_a = float((HEAD // 2) ** -0.5 * LOG2E)
    q_scale_b = float(HEAD ** -0.5 * LOG2E)

    prompt = past is None
    assert not prompt or tm == tq
    outs_a = _proj_call(x2, g_norm, w_bf, 0, d_a, q_scale_a, True, tm, prompt)
    outs_b = _proj_call(x2, g_norm, w_bf, 1, d_a, q_scale_b, False, tm, prompt,
                        logf_args=(wf_hl, b_pad, nh_b))
    qa, ka, va, ga = outs_a[:4]
    qb, kb, vb, gb = outs_b[:4]
    logf = outs_b[-1]

    r3 = lambda a: a.reshape(b, s, a.shape[-1])
    if prompt:
        c = _cumsum_call(r3(logf), cum_ch)
        ma = _self_attn_call("diff", r3(qa), r3(outs_a[4]), outs_a[5], r3(ga), _diff_tile_kq(tq),
                             (slopes, lamv, subg), tq=tq, hp=2, lam_init=lam_init)
        mb = _self_attn_call("fox", r3(qb), r3(outs_b[4]), outs_b[5], r3(gb), _causal_tile_kq(tq),
                             c, tq=tq, hp=2)
    else:
        cak, cav, cbk, cbv, cbf = past
        lp = cak.shape[1]
        assert lp % CHUNK == 0 and s == tq <= CHUNK
        tot = lp + s
        padded = -(-tot // cum_ch) * cum_ch
        lf_all = jnp.concatenate([cbf, r3(logf), jnp.zeros((b, padded - tot, nh_b), F32)], axis=1)
        c = jnp.transpose(_cumsum_call(lf_all, cum_ch), (0, 2, 1))
        slab = lambda a: a.reshape(b, lp * a.shape[2], HEAD)
        cak_t = jnp.transpose(cak, (0, 2, 3, 1))
        new = lambda a: a.reshape(b, -1, HEAD)
        ma = _cache_attn_call("diff", r3(qa), cak_t, slab(cav), ka.reshape((b, s) + ka.shape[1:]),
                              new(va), r3(ga), _diff_tile_qk(tq), (slopes, lamv, subg),
                              nh=nh_a, lp=lp, lam_init=lam_init)
        mb = _cache_attn_call("fox", r3(qb), slab(cbk), slab(cbv), new(kb), new(vb), r3(gb),
                              _causal_tile_qk(tq), (c[:, :, :lp], c[:, :, lp:tot]),
                              nh=nh_b, lp=lp)

    y = _out_call(x2, ma.reshape(b * s, -1), mb.reshape(b * s, -1), w_out, g_final,
                  min(tm, 512))
    return y.reshape(b, s, d), ka, va, kb, vb, logf


def kernel(x_prompt, x_sample, cache_a_k, cache_a_v, cache_b_k, cache_b_v, cache_b_logf,
           g_norm, w_in, b_f, lam_q1, lam_k1, lam_q2, lam_k2, sub_gain, w_out, g_final):
    depth, d = g_norm.shape
    assert depth == 1
    nh_a2, hqk = cache_a_k.shape[3], cache_a_k.shape[4]
    nh_a = nh_a2 // 2
    nh_b = cache_b_k.shape[3]
    d_a = nh_a2 * hqk
    assert 2 * hqk == HEAD and cache_b_k.shape[4] == HEAD and d == 2 * d_a
    lyr = 0
    lam_init = 0.8 - 0.6 * math.exp(-0.3 * lyr)

    w = jnp.transpose(w_in[lyr])
    w_bf = w.astype(BF16)
    wf = jnp.pad(w[8 * d_a:], ((0, LANES - nh_b), (0, 0)))
    wf_hi = wf.astype(BF16)
    wf_hl = jnp.concatenate([wf_hi, (wf - wf_hi.astype(F32)).astype(BF16)], axis=0)
    b_pad = jnp.pad(b_f[lyr].astype(F32), (0, LANES - nh_b)).reshape(1, LANES)
    slopes = 2.0 ** (-8.0 * jnp.arange(1, nh_a + 1, dtype=F32) / nh_a)
    lamv = jnp.pad(jnp.stack([lam_q1[lyr], lam_k1[lyr], lam_q2[lyr], lam_k2[lyr]]).astype(F32),
                   ((0, 0), (0, LANES - hqk)))
    subg = sub_gain[lyr].astype(F32).reshape(1, HEAD)
    params = (g_norm[lyr].reshape(1, d), w_bf, wf_hl, b_pad, slopes, lamv, subg, lam_init,
              w_out[lyr], g_final.reshape(1, d), d_a, nh_b)

    bp, sp, _ = x_prompt.shape
    bs, ss, _ = x_sample.shape
    yp, pak, pav, pbk, pbv, pbf = _layer(x_prompt, None, params, tm=512, tq=512, cum_ch=256)
    past = (cache_a_k[lyr], cache_a_v[lyr], cache_b_k[lyr], cache_b_v[lyr], cache_b_logf[lyr])
    ys, sak, sav, sbk, sbv, sbf = _layer(x_sample, past, params, tm=bs * ss, tq=ss, cum_ch=256)

    shp = lambda a, bb, s_, n, e: a.reshape(1, bb, s_, n, e)
    return (yp, ys,
            shp(pak, bp, sp, nh_a2, hqk), shp(pav, bp, sp, nh_a, HEAD),
            shp(pbk, bp, sp, nh_b, HEAD), shp(pbv, bp, sp, nh_b, HEAD),
            pbf.reshape(1, bp, sp, nh_b),
            shp(sak, bs, ss, nh_a2, hqk), shp(sav, bs, ss, nh_a, HEAD),
            shp(sbk, bs, ss, nh_b, HEAD), shp(sbv, bs, ss, nh_b, HEAD),
            sbf.reshape(1, bs, ss, nh_b))
```

```python
import functools
import math

import jax
import jax.numpy as jnp
from jax import lax
from jax.experimental import pallas as pl
from jax.experimental.pallas import tpu as pltpu

F32 = jnp.float32
BF16 = jnp.bfloat16

EPS = 1e-6
CHUNK = 64
HEAD = 128
NEG = -1e30
LANES = 128
LOG2E = math.log2(math.e)
N_BIAS = 3
QCOLS = 256
VMEM_LIMIT = 56 * 1024 * 1024

_NT = (((1,), (1,)), ((), ()))


def _cparams(n_axes):
    return pltpu.CompilerParams(
        dimension_semantics=("arbitrary",) * n_axes, vmem_limit_bytes=VMEM_LIMIT)


def _rms_scale(x):
    return lax.rsqrt(jnp.mean(x * x, axis=-1, keepdims=True) + EPS)


def _split3(x):
    x1 = x.astype(BF16)
    r1 = x - x1.astype(F32)
    x2 = r1.astype(BF16)
    x3 = (r1 - x2.astype(F32)).astype(BF16)
    return x1, x2, x3


def _proj_kernel(*refs, q_scale, width, k_maps, attn_copies, nh_logf):
    x_ref, g_ref, w_ref = refs[:3]
    n_out = 4 + (2 if attn_copies else 0) + (1 if nh_logf else 0)
    o_refs = refs[-n_out:]
    q_ref, k_ref, v_ref, gate_ref = o_refs[:4]
    x = x_ref[...]
    xn32 = x * _rms_scale(x) * g_ref[...]
    xn = xn32.astype(BF16)
    slab = lambda j: lax.dot_general(xn, w_ref[j * width:(j + 1) * width, :], _NT,
                                     preferred_element_type=F32)
    nh = width // HEAD

    def store_heads(o_ref, z):
        for h in range(nh):
            o_ref[pl.ds(h, z.shape[0], stride=nh), :] = z[:, h * HEAD:(h + 1) * HEAD]

    q_ref[...] = (slab(0) * q_scale).astype(q_ref.dtype)
    k = slab(1)
    if k_maps == "transposed":
        for j in range(2 * nh):
            k_ref[0, j] = k[:, j * (HEAD // 2):(j + 1) * (HEAD // 2)].T
    elif k_maps:
        k_ref[...] = k.reshape(k_ref.shape)
    else:
        store_heads(k_ref, k)
    v = slab(2)
    store_heads(v_ref, v)
    gate_ref[...] = slab(3).astype(gate_ref.dtype)
    if attn_copies:
        kbf_ref, vt_ref = o_refs[4:6]
        kbf_ref[...] = k.astype(BF16)
        for h in range(nh):
            vt_ref[h, 0] = v[:, h * HEAD:(h + 1) * HEAD].T.astype(BF16)
    if nh_logf:
        whl_ref, b_ref = refs[3:5]
        lo = (xn32 - xn.astype(F32)).astype(BF16)
        both = lax.dot_general(xn, whl_ref[...], _NT, preferred_element_type=F32)
        fl = (both[:, :LANES] + both[:, LANES:]
              + lax.dot_general(lo, whl_ref[:LANES, :], _NT, preferred_element_type=F32))
        t = fl + b_ref[...]
        logf = jnp.minimum(t, 0.0) - jnp.log(1.0 + jnp.exp(-jnp.abs(t)))
        o_refs[-1][...] = logf[:, :nh_logf]


def _proj_call(x2, g, w_bf, group, width, q_scale, k_maps, tm, attn_copies, logf_args=None,
               k_maps_tokens=None):
    m, d = x2.shape
    const = lambda i: (0, 0)
    row = lambda i: (i, 0)
    nh = width // HEAD
    heads = pl.BlockSpec((tm * nh, HEAD), row)
    heads_shape = jax.ShapeDtypeStruct((m * nh, HEAD), F32)
    maps = pl.BlockSpec((tm, 2 * nh, HEAD // 2), lambda i: (i, 0, 0))
    maps_shape = jax.ShapeDtypeStruct((m, 2 * nh, HEAD // 2), F32)
    in_specs = [pl.BlockSpec((tm, d), row),
                pl.BlockSpec((1, d), const),
                pl.BlockSpec((4 * width, d), lambda i: (group, 0), pipeline_mode=pl.Buffered(1))]
    args = [x2, g, w_bf]
    flat = pl.BlockSpec((tm, width), row)
    if k_maps == "transposed":
        tps = k_maps_tokens // tm
        maps = pl.BlockSpec((1, 2 * nh, HEAD // 2, tm), lambda i: (i // tps, 0, 0, i % tps))
        maps_shape = jax.ShapeDtypeStruct((m // k_maps_tokens, 2 * nh, HEAD // 2, k_maps_tokens), F32)
    out_specs = [flat, maps if k_maps else heads, heads, flat]
    out_shape = [jax.ShapeDtypeStruct((m, width), BF16),
                 maps_shape if k_maps else heads_shape,
                 heads_shape,
                 jax.ShapeDtypeStruct((m, width), BF16)]
    if attn_copies:
        out_specs += [flat, pl.BlockSpec((nh, 1, HEAD, tm), lambda i: (0, i, 0, 0))]
        out_shape += [jax.ShapeDtypeStruct((m, width), BF16),
                      jax.ShapeDtypeStruct((nh, m // tm, HEAD, tm), BF16)]
    nh_logf = 0
    if logf_args is not None:
        wf_hl, b_pad, nh_logf = logf_args
        in_specs += [pl.BlockSpec((2 * LANES, d), const), pl.BlockSpec((1, LANES), const)]
        args += [wf_hl, b_pad]
        out_specs.append(pl.BlockSpec((tm, nh_logf), row))
        out_shape.append(jax.ShapeDtypeStruct((m, nh_logf), F32))
    return pl.pallas_call(
        functools.partial(_proj_kernel, q_scale=q_scale, width=width, k_maps=k_maps,
                          attn_copies=attn_copies, nh_logf=nh_logf),
        grid=(m // tm,),
        in_specs=in_specs,
        out_specs=out_specs,
        out_shape=out_shape,
        compiler_params=_cparams(1),
        name=f"proj_group{group}",
    )(*args)


def _cumsum_kernel(x_ref, l_ref, o_ref, *, ch):
    ltri = l_ref[...]
    carry = jnp.zeros((1, x_ref.shape[1]), F32)
    for c in range(x_ref.shape[0] // ch):
        x1, x2, x3 = _split3(x_ref[c * ch:(c + 1) * ch, :])
        y = (jnp.dot(ltri, x1, preferred_element_type=F32)
             + jnp.dot(ltri, x2, preferred_element_type=F32)
             + jnp.dot(ltri, x3, preferred_element_type=F32)) + carry
        o_ref[c * ch:(c + 1) * ch, :] = y
        carry = y[ch - 1:ch, :]


def _cumsum_call(x, ch):
    b, s, nh = x.shape
    cols = jnp.transpose(x, (1, 0, 2)).reshape(s, b * nh)
    tri = (jnp.arange(ch)[:, None] >= jnp.arange(ch)[None, :]).astype(BF16)
    out = pl.pallas_call(
        functools.partial(_cumsum_kernel, ch=ch),
        grid=(1,),
        in_specs=[pl.BlockSpec((s, b * nh), lambda i: (0, 0)),
                  pl.BlockSpec((ch, ch), lambda i: (0, 0))],
        out_specs=pl.BlockSpec((s, b * nh), lambda i: (0, 0)),
        out_shape=jax.ShapeDtypeStruct((s, b * nh), F32),
        compiler_params=_cparams(1),
        name="cumsum",
    )(cols, tri)
    return jnp.transpose(out.reshape(s, b, nh), (1, 0, 2))


def _lambda(lamv_ref, lam_init):
    lv = lamv_ref[...]
    return (jnp.exp(jnp.sum(lv[0:1] * lv[1:2], axis=1, keepdims=True))
            - jnp.exp(jnp.sum(lv[2:3] * lv[3:4], axis=1, keepdims=True)) + lam_init)


def _gate(g):
    g = g.astype(F32)
    return g / (1.0 + jnp.exp(-g))


def _stack_maps(q):
    lane = lax.broadcasted_iota(jnp.int32, q.shape, 1)
    zero = jnp.zeros_like(q)
    return jnp.concatenate([jnp.where(lane < HEAD // 2, q, zero),
                            jnp.where(lane >= HEAD // 2, q, zero)], axis=0)


def _self_attn_kernel(*refs, mode, tq, hp, lam_init):
    it = iter(refs)
    q_ref, k_ref, vt_ref, g_ref, tile_ref = next(it), next(it), next(it), next(it), next(it)
    if mode == "diff":
        slope_ref, lamv_ref, subg_ref = next(it), next(it), next(it)
    else:
        c_ref = next(it)
    o_ref = next(it)
    kaug_ref, qaug_ref, m_ref, l_ref, acc_ref, sa_ref, sb_ref = (next(it) for _ in range(7))

    hg = pl.program_id(1)
    nq = vt_ref.shape[1]
    tk = tq
    n = qaug_ref.shape[1]
    nhead = n // hp
    npairs = nq // 2
    if mode == "diff":
        bscales = [slope_ref[hg * hp + hs] * LOG2E for hs in range(hp)]

    lane = lax.broadcasted_iota(jnp.int32, (tk, HEAD), 1)
    qlane = lax.broadcasted_iota(jnp.int32, (nhead, HEAD), 1)
    for hs in range(hp):
        if mode == "diff":
            s1, s2, s3 = (piece.astype(F32)
                          for piece in _split3(jnp.full((1, HEAD), bscales[hs], F32)))
            ones = jnp.where(qlane < 2, s1, jnp.where(qlane < 4, s2,
                                                      jnp.where(qlane < 6, s3, 0.0))).astype(BF16)
        else:
            ones = jnp.where(qlane < N_BIAS, 1.0, 0.0).astype(BF16)
        hl = slice(hs * HEAD, (hs + 1) * HEAD)
        for j in range(nq):
            rows = slice(j * tk, (j + 1) * tk)
            if mode == "diff":
                pos = lax.broadcasted_iota(jnp.int32, (tk, HEAD), 0) + j * tk
                piece = jnp.where(lane % 2 == 0, (pos // 16) * 16, pos % 16)
                extra = jnp.where(lane < 6, piece, 0).astype(F32)
            else:
                c8 = c_ref[0, rows, :]
                sel = lax.broadcasted_iota(jnp.int32, c8.shape, 1) == hg * hp + hs
                bias = -LOG2E * jnp.sum(jnp.where(sel, c8, 0.0), axis=1, keepdims=True)
                b1, b2, b3 = (piece.astype(F32) for piece in _split3(bias))
                extra = jnp.where(lane == 0, b1, jnp.where(lane == 1, b2,
                                                           jnp.where(lane == 2, b3, 0.0)))
            kaug_ref[hs, rows, :HEAD] = k_ref[0, rows, hl]
            kaug_ref[hs, rows, HEAD:] = extra.astype(BF16)
            q = q_ref[0, rows, hl]
            qs = _stack_maps(q) if mode == "diff" else q
            qaug_ref[j, hs * nhead:(hs + 1) * nhead, :] = jnp.concatenate([qs, ones], axis=1)
    m_ref[...] = jnp.full(m_ref.shape, NEG, F32)
    l_ref[...] = jnp.zeros(l_ref.shape, F32)
    acc_ref[...] = jnp.zeros(acc_ref.shape, F32)

    groups = [(c0 // nhead, slice(c0, c0 + QCOLS)) for c0 in range(0, n, QCOLS)]

    tile = tile_ref[...]
    tiles = [bscales[hs] * tile if mode == "diff" else tile for hs in range(hp)]

    def product(p, s):
        if isinstance(s, int) and s == nq - 1:
            return p, p
        if isinstance(s, int) and s == nq:
            return nq - 1 - p, nq - 1 - p
        first = s < p
        return jnp.where(first, p, nq - 1 - p), jnp.where(first, s, s - p)

    def is_diag(s):
        return isinstance(s, int) and s >= nq - 1

    def nkeys(s, group):
        low_queries = (group[1].start % nhead) % tq + QCOLS <= tq // 2
        return tk // 2 if is_diag(s) and low_queries else tk

    def scores(p, s, group):
        hs, cols = group
        qb, j = product(p, s)
        nk = nkeys(s, group)
        k0 = j * tk if isinstance(j, int) else pl.multiple_of(j * tk, tk)
        sc = lax.dot_general(kaug_ref[hs, pl.ds(k0, nk), :], qaug_ref[qb, cols, :], _NT,
                             preferred_element_type=F32)
        if is_diag(s):
            t0 = (cols.start % nhead) % tq
            sc = sc + tiles[hs][:nk, t0:t0 + QCOLS]
        return sc

    def consume(s_ref, p, s, group):
        hs, cols = group
        qb, j = product(p, s)
        nk = nkeys(s, group)
        sc = s_ref[:nk, cols]
        m_prev = m_ref[qb, :, cols]
        m_new = jnp.maximum(m_prev, jnp.max(sc, axis=0, keepdims=True))
        alpha = jnp.exp2(m_prev - m_new)
        pr = jnp.exp2(sc - m_new)
        l_ref[qb, :, cols] = alpha * l_ref[qb, :, cols] + jnp.sum(pr, axis=0, keepdims=True)
        acc_ref[qb, :, cols] = alpha * acc_ref[qb, :, cols] + jnp.dot(
            vt_ref[hs, j, :, :nk], pr.astype(BF16), preferred_element_type=F32)
        m_ref[qb, :, cols] = m_new

    def pipe_step(nxt, cur, nxt_ref, cur_ref):
        for gi, group in enumerate(groups):
            sc = scores(*nxt, group)
            nxt_ref[:sc.shape[0], group[1]] = sc
            if cur is not None and gi >= 1:
                consume(cur_ref, *cur, groups[gi - 1])
        if cur is not None:
            consume(cur_ref, *cur, groups[-1])

    def run_pair(p, a_ref, b_ref):
        def two_slots(i, carry):
            pipe_step((p, 2 * i + 1), (p, 2 * i), b_ref, a_ref)
            pipe_step((p, 2 * i + 2), (p, 2 * i + 1), a_ref, b_ref)
            return carry

        lax.fori_loop(0, (nq - 2) // 2, two_slots, 0)
        pipe_step((p, nq - 1), (p, nq - 2), b_ref, a_ref)
        pipe_step((p, nq), (p, nq - 1), a_ref, b_ref)
        pipe_step((jnp.minimum(p + 1, npairs - 1), 0), (p, nq), b_ref, a_ref)

    pipe_step((0, 0), None, sa_ref, None)

    def two_pairs(i, carry):
        run_pair(2 * i, sa_ref, sb_ref)
        run_pair(2 * i + 1, sb_ref, sa_ref)
        return carry

    lax.fori_loop(0, npairs // 2, two_pairs, 0)

    def finalize_block(blk):
        rows = pl.ds(pl.multiple_of(blk * tq, tq), tq)
        for hs in range(hp):
            hl = slice(hs * HEAD, (hs + 1) * HEAD)
            hc = slice(hs * nhead, (hs + 1) * nhead)
            ot = acc_ref[blk, :, hc] * (1.0 / l_ref[blk, :, hc])
            if mode == "diff":
                ot = ot[:, :tq] - _lambda(lamv_ref, lam_init) * ot[:, tq:]
            o = ot.T
            if mode == "diff":
                o = o * _rms_scale(o) * subg_ref[...] * (1.0 - lam_init)
            o_ref[0, rows, hl] = (_gate(g_ref[0, rows, hl]) * o).astype(o_ref.dtype)

    def finalize_two(i, carry):
        finalize_block(2 * i)
        finalize_block(2 * i + 1)
        return carry

    lax.fori_loop(0, nq // 2, finalize_two, 0)


def _self_attn_call(mode, q, k, vt, g, tile, extra, *, tq, hp, lam_init=0.0):
    b, t, w = q.shape
    nh = w // HEAD
    nq = t // tq
    assert nq % 4 == 0 and nh % hp == 0
    assert vt.shape == (nh, b * nq, HEAD, tq)
    n = hp * (2 * tq if mode == "diff" else tq)
    seq_spec = pl.BlockSpec((1, t, hp * HEAD), lambda bi, hi: (bi, 0, hi))
    const2 = lambda bi, hi: (0, 0)
    in_specs = [seq_spec, seq_spec,
                pl.BlockSpec((hp, nq, HEAD, tq), lambda bi, hi: (hi, bi, 0, 0)),
                seq_spec,
                pl.BlockSpec((tq, tq), const2, pipeline_mode=pl.Buffered(1))]
    args = [q, k, vt, g, tile]
    if mode == "diff":
        slopes, lamv, subg = extra
        in_specs += [pl.BlockSpec(memory_space=pltpu.SMEM),
                     pl.BlockSpec((4, LANES), const2),
                     pl.BlockSpec((1, HEAD), const2)]
        args += [slopes, lamv, subg]
    else:
        c = extra
        in_specs += [pl.BlockSpec((1, t, c.shape[2]), lambda bi, hi: (bi, 0, 0),
                                  pipeline_mode=pl.Buffered(1))]
        args += [c]
    return pl.pallas_call(
        functools.partial(_self_attn_kernel, mode=mode, tq=tq, hp=hp, lam_init=lam_init),
        grid=(b, nh // hp),
        in_specs=in_specs,
        out_specs=seq_spec,
        out_shape=jax.ShapeDtypeStruct((b, t, w), BF16),
        scratch_shapes=[pltpu.VMEM((hp, t, 2 * HEAD), BF16),
                        pltpu.VMEM((nq, n, 2 * HEAD), BF16),
                        pltpu.VMEM((nq, 1, n), F32), pltpu.VMEM((nq, 1, n), F32),
                        pltpu.VMEM((nq, HEAD, n), F32),
                        pltpu.VMEM((tq, n), F32), pltpu.VMEM((tq, n), F32)],
        compiler_params=_cparams(2),
        name=f"attn_{mode}_self",
    )(*args)


def _cache_attn_kernel(*refs, mode, nh, lp, lam_init):
    it = iter(refs)
    q_ref, kc_ref, vc_ref, kn_ref, vn_ref, g_ref, tile_ref = (next(it) for _ in range(7))
    if mode == "diff":
        slope_ref, lamv_ref, subg_ref = next(it), next(it), next(it)
    else:
        cp_ref, cd_ref = next(it), next(it)
    o_ref = next(it)
    tq = q_ref.shape[1]
    tile = tile_ref[...]
    col = lax.broadcasted_iota(jnp.int32, (1, lp), 1).astype(F32)
    if mode == "diff":
        kn_all = kn_ref[0].reshape(tq, nh * HEAD).astype(BF16)

    for h in range(nh):
        lanes = slice(h * HEAD, (h + 1) * HEAD)
        heads = pl.ds(h, lp, stride=nh)
        q = q_ref[0, :, lanes]
        new_heads = pl.ds(h, tq, stride=nh)
        vn = vn_ref[0, new_heads, :].astype(BF16)
        vc = vc_ref[0, heads, :].astype(BF16)
        if mode == "diff":
            qs = _stack_maps(q)
            kc_t = kc_ref[0, 2 * h:2 * h + 2].reshape(HEAD, lp).astype(BF16)
            kn = kn_all[:, lanes]
            bscale = slope_ref[h] * LOG2E
            bias_c = bscale * (col - float(lp))
            bias_n = bscale * jnp.concatenate([tile, tile], axis=0)
        else:
            qs = q
            kc = kc_ref[0, heads, :].astype(BF16)
            kn = kn_ref[0, new_heads, :].astype(BF16)
            cn = cd_ref[0, h:h + 1, :]
            cref = cn[:, 0:1]
            bias_c = LOG2E * (cref - cp_ref[0, h:h + 1, :])
            bias_n = tile + LOG2E * (cref - cn)
        if mode == "diff":
            s_c = jnp.dot(qs, kc_t, preferred_element_type=F32) + bias_c
        else:
            s_c = lax.dot_general(qs, kc, _NT, preferred_element_type=F32) + bias_c
        s_n = lax.dot_general(qs, kn, _NT, preferred_element_type=F32) + bias_n
        m = jnp.maximum(jnp.max(s_c, axis=1, keepdims=True), jnp.max(s_n, axis=1, keepdims=True))
        p_c = jnp.exp2(s_c - m)
        p_n = jnp.exp2(s_n - m)
        l = jnp.sum(p_c, axis=1, keepdims=True) + jnp.sum(p_n, axis=1, keepdims=True)
        acc = (jnp.dot(p_c.astype(BF16), vc, preferred_element_type=F32)
               + jnp.dot(p_n.astype(BF16), vn, preferred_element_type=F32))
        o = acc / l
        if mode == "diff":
            o = o[:tq] - _lambda(lamv_ref, lam_init) * o[tq:]
            o = o * _rms_scale(o) * subg_ref[...] * (1.0 - lam_init)
        o_ref[0, :, lanes] = (_gate(g_ref[0, :, lanes]) * o).astype(o_ref.dtype)


def _cache_attn_call(mode, q, kc, vc, kn, vn, g, tile, extra, *, nh, lp, lam_init=0.0):
    b, tq, w = q.shape
    row3 = lambda bi: (bi, 0, 0)
    const2 = lambda bi: (0, 0)
    new_spec = pl.BlockSpec((1, tq, w), row3)
    whole = lambda a: pl.BlockSpec((1,) + a.shape[1:], lambda bi: (bi,) + (0,) * (a.ndim - 1))
    in_specs = [new_spec, whole(kc), whole(vc), whole(kn), whole(vn), new_spec,
                pl.BlockSpec((tq, tq), const2)]
    args = [q, kc, vc, kn, vn, g, tile]
    if mode == "diff":
        slopes, lamv, subg = extra
        in_specs += [pl.BlockSpec(memory_space=pltpu.SMEM),
                     pl.BlockSpec((4, LANES), const2),
                     pl.BlockSpec((1, HEAD), const2)]
        args += [slopes, lamv, subg]
    else:
        cp, cd = extra
        in_specs += [pl.BlockSpec((1,) + cp.shape[1:], row3),
                     pl.BlockSpec((1,) + cd.shape[1:], row3)]
        args += [cp, cd]
    return pl.pallas_call(
        functools.partial(_cache_attn_kernel, mode=mode, nh=nh, lp=lp, lam_init=lam_init),
        grid=(b,),
        in_specs=in_specs,
        out_specs=new_spec,
        out_shape=jax.ShapeDtypeStruct((b, tq, w), BF16),
        compiler_params=_cparams(1),
        name=f"attn_{mode}_cache",
    )(*args)


def _out_kernel(x_ref, ma_ref, mb_ref, w_ref, gf_ref, o_ref, wbf_ref):
    @pl.when(pl.program_id(0) == 0)
    def _():
        wbf_ref[...] = w_ref[...].astype(BF16)

    da = ma_ref.shape[1]
    hres = (x_ref[...]
            + jnp.dot(ma_ref[...], wbf_ref[:da, :], preferred_element_type=F32)
            + jnp.dot(mb_ref[...], wbf_ref[da:, :], preferred_element_type=F32))
    o_ref[...] = hres * _rms_scale(hres) * gf_ref[...]


def _out_call(x2, ma, mb, w, gf, tm):
    m, d = x2.shape
    da, db = ma.shape[1], mb.shape[1]
    return pl.pallas_call(
        _out_kernel,
        grid=(m // tm,),
        in_specs=[pl.BlockSpec((tm, d), lambda i: (i, 0)),
                  pl.BlockSpec((tm, da), lambda i: (i, 0)),
                  pl.BlockSpec((tm, db), lambda i: (i, 0)),
                  pl.BlockSpec((da + db, d), lambda i: (0, 0), pipeline_mode=pl.Buffered(1)),
                  pl.BlockSpec((1, d), lambda i: (0, 0))],
        out_specs=pl.BlockSpec((tm, d), lambda i: (i, 0)),
        out_shape=jax.ShapeDtypeStruct((m, d), F32),
        scratch_shapes=[pltpu.VMEM((da + db, d), BF16)],
        compiler_params=_cparams(1),
        name="out_proj",
    )(x2, ma, mb, w, gf)


def _chunk_allowed(q, k):
    return (k // CHUNK) <= (q // CHUNK)


def _diff_tile_kq(t):
    k = jnp.arange(t)[:, None]
    q = jnp.arange(t)[None, :]
    return jnp.where(_chunk_allowed(q, k), jnp.minimum(0, 2 * (q - k)).astype(F32), NEG)


def _causal_tile_kq(t):
    k = jnp.arange(t)[:, None]
    q = jnp.arange(t)[None, :]
    return jnp.where(k <= q, 0.0, NEG).astype(F32)


def _diff_tile_qk(t):
    q = jnp.arange(t)[:, None]
    k = jnp.arange(t)[None, :]
    return jnp.where(_chunk_allowed(q, k), (q - jnp.abs(q - k)).astype(F32), NEG)


def _causal_tile_qk(t):
    return _causal_tile_kq(t).T


def _layer(x, past, params, *, tm, tq, cum_ch):
    (g_norm, w_bf, wf_hl, b_pad, slopes, lamv, subg, lam_init,
     w_out, g_final, d_a, nh_b) = params
    b, s, d = x.shape
    nh_a = d_a // HEAD
    x2 = x.reshape(b * s, d)
    q_scale_a = float((HEAD // 2) ** -0.5 * LOG2E)
    q_scale_b = float(HEAD ** -0.5 * LOG2E)

    prompt = past is None
    assert not prompt or tm == tq
    outs_a = _proj_call(x2, g_norm, w_bf, 0, d_a, q_scale_a, "transposed" if prompt else True, tm,
                        prompt, k_maps_tokens=s)
    outs_b = _proj_call(x2, g_norm, w_bf, 1, d_a, q_scale_b, False, tm, prompt,
                        logf_args=(wf_hl, b_pad, nh_b))
    qa, ka, va, ga = outs_a[:4]
    qb, kb, vb, gb = outs_b[:4]
    logf = outs_b[-1]

    r3 = lambda a: a.reshape(b, s, a.shape[-1])
    if prompt:
        c = _cumsum_call(r3(logf), cum_ch)
        ma = _self_attn_call("diff", r3(qa), r3(outs_a[4]), outs_a[5], r3(ga), _diff_tile_kq(tq),
                             (slopes, lamv, subg), tq=tq, hp=2, lam_init=lam_init)
        mb = _self_attn_call("fox", r3(qb), r3(outs_b[4]), outs_b[5], r3(gb), _causal_tile_kq(tq),
                             c, tq=tq, hp=2)
    else:
        cak, cav, cbk, cbv, cbf = past
        lp = cak.shape[1]
        assert lp % CHUNK == 0 and s == tq <= CHUNK
        tot = lp + s
        padded = -(-tot // cum_ch) * cum_ch
        lf_all = jnp.concatenate([cbf, r3(logf), jnp.zeros((b, padded - tot, nh_b), F32)], axis=1)
        c = jnp.transpose(_cumsum_call(lf_all, cum_ch), (0, 2, 1))
        slab = lambda a: a.reshape(b, lp * a.shape[2], HEAD)
        cak_t = jnp.transpose(cak, (0, 2, 3, 1))
        new = lambda a: a.reshape(b, -1, HEAD)
        ma = _cache_attn_call("diff", r3(qa), cak_t, slab(cav), ka.reshape((b, s) + ka.shape[1:]),
                              new(va), r3(ga), _diff_tile_qk(tq), (slopes, lamv, subg),
                              nh=nh_a, lp=lp, lam_init=lam_init)
        mb = _cache_attn_call("fox", r3(qb), slab(cbk), slab(cbv), new(kb), new(vb), r3(gb),
                              _causal_tile_qk(tq), (c[:, :, :lp], c[:, :, lp:tot]),
                              nh=nh_b, lp=lp)

    y = _out_call(x2, ma.reshape(b * s, -1), mb.reshape(b * s, -1), w_out, g_final,
                  min(tm, 512))
    if prompt:
        ka = jnp.transpose(ka, (0, 3, 1, 2))
    return y.reshape(b, s, d), ka, va, kb, vb, logf


def kernel(x_prompt, x_sample, cache_a_k, cache_a_v, cache_b_k, cache_b_v, cache_b_logf,
           g_norm, w_in, b_f, lam_q1, lam_k1, lam_q2, lam_k2, sub_gain, w_out, g_final):
    depth, d = g_norm.shape
    assert depth == 1
    nh_a2, hqk = cache_a_k.shape[3], cache_a_k.shape[4]
    nh_a = nh_a2 // 2
    nh_b = cache_b_k.shape[3]
    d_a = nh_a2 * hqk
    assert 2 * hqk == HEAD and cache_b_k.shape[4] == HEAD and d == 2 * d_a
    lyr = 0
    lam_init = 0.8 - 0.6 * math.exp(-0.3 * lyr)

    w = jnp.transpose(w_in[lyr])
    w_bf = w.astype(BF16)
    wf = jnp.pad(w[8 * d_a:], ((0, LANES - nh_b), (0, 0)))
    wf_hi = wf.astype(BF16)
    wf_hl = jnp.concatenate([wf_hi, (wf - wf_hi.astype(F32)).astype(BF16)], axis=0)
    b_pad = jnp.pad(b_f[lyr].astype(F32), (0, LANES - nh_b)).reshape(1, LANES)
    slopes = 2.0 ** (-8.0 * jnp.arange(1, nh_a + 1, dtype=F32) / nh_a)
    lamv = jnp.pad(jnp.stack([lam_q1[lyr], lam_k1[lyr], lam_q2[lyr], lam_k2[lyr]]).astype(F32),
                   ((0, 0), (0, LANES - hqk)))
    subg = sub_gain[lyr].astype(F32).reshape(1, HEAD)
    params = (g_norm[lyr].reshape(1, d), w_bf, wf_hl, b_pad, slopes, lamv, subg, lam_init,
              w_out[lyr], g_final.reshape(1, d), d_a, nh_b)

    bp, sp, _ = x_prompt.shape
    bs, ss, _ = x_sample.shape
    yp, pak, pav, pbk, pbv, pbf = _layer(x_prompt, None, params, tm=512, tq=512, cum_ch=256)
    past = (cache_a_k[lyr], cache_a_v[lyr], cache_b_k[lyr], cache_b_v[lyr], cache_b_logf[lyr])
    ys, sak, sav, sbk, sbv, sbf = _layer(x_sample, past, params, tm=bs * ss, tq=ss, cum_ch=256)

    shp = lambda a, bb, s_, n, e: a.reshape(1, bb, s_, n, e)
    return (yp, ys,
            shp(pak, bp, sp, nh_a2, hqk), shp(pav, bp, sp, nh_a, HEAD),
            shp(pbk, bp, sp, nh_b, HEAD), shp(pbv, bp, sp, nh_b, HEAD),
            pbf.reshape(1, bp, sp, nh_b),
            shp(sak, bs, ss, nh_a2, hqk), shp(sav, bs, ss, nh_a, HEAD),
            shp(sbk, bs, ss, nh_b, HEAD), shp(sbv, bs, ss, nh_b, HEAD),
            sbf.reshape(1, bs, ss, nh_b))
```

```python
import functools
import math

import jax
import jax.numpy as jnp
from jax import lax
from jax.experimental import pallas as pl
from jax.experimental.pallas import tpu as pltpu

F32 = jnp.float32
BF16 = jnp.bfloat16

EPS = 1e-6
CHUNK = 64
HEAD = 128
NEG = -1e30
LANES = 128
LOG2E = math.log2(math.e)
N_BIAS = 3
QCOLS = 256
VMEM_LIMIT = 56 * 1024 * 1024

_NT = (((1,), (1,)), ((), ()))


def _cparams(n_axes):
    return pltpu.CompilerParams(
        dimension_semantics=("arbitrary",) * n_axes, vmem_limit_bytes=VMEM_LIMIT)


def _rms_scale(x):
    return lax.rsqrt(jnp.mean(x * x, axis=-1, keepdims=True) + EPS)


def _split3(x):
    x1 = x.astype(BF16)
    r1 = x - x1.astype(F32)
    x2 = r1.astype(BF16)
    x3 = (r1 - x2.astype(F32)).astype(BF16)
    return x1, x2, x3


def _proj_kernel(*refs, q_scale, width, k_maps, attn_copies, nh_logf):
    x_ref, g_ref, w_ref = refs[:3]
    n_out = 4 + (2 if attn_copies else 0) + (1 if nh_logf else 0)
    o_refs = refs[-n_out:]
    q_ref, k_ref, v_ref, gate_ref = o_refs[:4]
    x = x_ref[...]
    xn32 = x * _rms_scale(x) * g_ref[...]
    xn = xn32.astype(BF16)
    slab = lambda j: lax.dot_general(xn, w_ref[j * width:(j + 1) * width, :], _NT,
                                     preferred_element_type=F32)
    nh = width // HEAD

    def store_heads(o_ref, z):
        for h in range(nh):
            o_ref[pl.ds(h, z.shape[0], stride=nh), :] = z[:, h * HEAD:(h + 1) * HEAD]

    q_ref[...] = (slab(0) * q_scale).astype(q_ref.dtype)
    k = slab(1)
    if k_maps == "transposed":
        for j in range(2 * nh):
            k_ref[0, j] = k[:, j * (HEAD // 2):(j + 1) * (HEAD // 2)].T
    elif k_maps:
        k_ref[...] = k.reshape(k_ref.shape)
    else:
        store_heads(k_ref, k)
    v = slab(2)
    store_heads(v_ref, v)
    gate_ref[...] = slab(3).astype(gate_ref.dtype)
    if attn_copies:
        kbf_ref, vt_ref = o_refs[4:6]
        kbf_ref[...] = k.astype(BF16)
        for h in range(nh):
            vt_ref[h, 0] = v[:, h * HEAD:(h + 1) * HEAD].T.astype(BF16)
    if nh_logf:
        whl_ref, b_ref = refs[3:5]
        lo = (xn32 - xn.astype(F32)).astype(BF16)
        both = lax.dot_general(xn, whl_ref[...], _NT, preferred_element_type=F32)
        fl = (both[:, :LANES] + both[:, LANES:]
              + lax.dot_general(lo, whl_ref[:LANES, :], _NT, preferred_element_type=F32))
        t = fl + b_ref[...]
        logf = jnp.minimum(t, 0.0) - jnp.log(1.0 + jnp.exp(-jnp.abs(t)))
        o_refs[-1][...] = logf[:, :nh_logf]


def _proj_call(x2, g, w_bf, group, width, q_scale, k_maps, tm, attn_copies, logf_args=None,
               k_maps_tokens=None):
    m, d = x2.shape
    const = lambda i: (0, 0)
    row = lambda i: (i, 0)
    nh = width // HEAD
    heads = pl.BlockSpec((tm * nh, HEAD), row)
    heads_shape = jax.ShapeDtypeStruct((m * nh, HEAD), F32)
    maps = pl.BlockSpec((tm, 2 * nh, HEAD // 2), lambda i: (i, 0, 0))
    maps_shape = jax.ShapeDtypeStruct((m, 2 * nh, HEAD // 2), F32)
    in_specs = [pl.BlockSpec((tm, d), row),
                pl.BlockSpec((1, d), const),
                pl.BlockSpec((4 * width, d), lambda i: (group, 0), pipeline_mode=pl.Buffered(1))]
    args = [x2, g, w_bf]
    flat = pl.BlockSpec((tm, width), row)
    if k_maps == "transposed":
        tps = k_maps_tokens // tm
        maps = pl.BlockSpec((1, 2 * nh, HEAD // 2, tm), lambda i: (i // tps, 0, 0, i % tps))
        maps_shape = jax.ShapeDtypeStruct((m // k_maps_tokens, 2 * nh, HEAD // 2, k_maps_tokens), F32)
    out_specs = [flat, maps if k_maps else heads, heads, flat]
    out_shape = [jax.ShapeDtypeStruct((m, width), BF16),
                 maps_shape if k_maps else heads_shape,
                 heads_shape,
                 jax.ShapeDtypeStruct((m, width), BF16)]
    if attn_copies:
        out_specs += [flat, pl.BlockSpec((nh, 1, HEAD, tm), lambda i: (0, i, 0, 0))]
        out_shape += [jax.ShapeDtypeStruct((m, width), BF16),
                      jax.ShapeDtypeStruct((nh, m // tm, HEAD, tm), BF16)]
    nh_logf = 0
    if logf_args is not None:
        wf_hl, b_pad, nh_logf = logf_args
        in_specs += [pl.BlockSpec((2 * LANES, d), const), pl.BlockSpec((1, LANES), const)]
        args += [wf_hl, b_pad]
        out_specs.append(pl.BlockSpec((tm, nh_logf), row))
        out_shape.append(jax.ShapeDtypeStruct((m, nh_logf), F32))
    return pl.pallas_call(
        functools.partial(_proj_kernel, q_scale=q_scale, width=width, k_maps=k_maps,
                          attn_copies=attn_copies, nh_logf=nh_logf),
        grid=(m // tm,),
        in_specs=in_specs,
        out_specs=out_specs,
        out_shape=out_shape,
        compiler_params=_cparams(1),
        name=f"proj_group{group}",
    )(*args)


def _cumsum_kernel(x_ref, l_ref, o_ref, *, ch):
    ltri = l_ref[...]
    carry = jnp.zeros((1, x_ref.shape[1]), F32)
    for c in range(x_ref.shape[0] // ch):
        x1, x2, x3 = _split3(x_ref[c * ch:(c + 1) * ch, :])
        y = (jnp.dot(ltri, x1, preferred_element_type=F32)
             + jnp.dot(ltri, x2, preferred_element_type=F32)
             + jnp.dot(ltri, x3, preferred_element_type=F32)) + carry
        o_ref[c * ch:(c + 1) * ch, :] = y
        carry = y[ch - 1:ch, :]


def _cumsum_call(x, ch):
    b, s, nh = x.shape
    cols = jnp.transpose(x, (1, 0, 2)).reshape(s, b * nh)
    tri = (jnp.arange(ch)[:, None] >= jnp.arange(ch)[None, :]).astype(BF16)
    out = pl.pallas_call(
        functools.partial(_cumsum_kernel, ch=ch),
        grid=(1,),
        in_specs=[pl.BlockSpec((s, b * nh), lambda i: (0, 0)),
                  pl.BlockSpec((ch, ch), lambda i: (0, 0))],
        out_specs=pl.BlockSpec((s, b * nh), lambda i: (0, 0)),
        out_shape=jax.ShapeDtypeStruct((s, b * nh), F32),
        compiler_params=_cparams(1),
        name="cumsum",
    )(cols, tri)
    return jnp.transpose(out.reshape(s, b, nh), (1, 0, 2))


def _lambda(lamv_ref, lam_init):
    lv = lamv_ref[...]
    return (jnp.exp(jnp.sum(lv[0:1] * lv[1:2], axis=1, keepdims=True))
            - jnp.exp(jnp.sum(lv[2:3] * lv[3:4], axis=1, keepdims=True)) + lam_init)


def _gate(g):
    g = g.astype(F32)
    return g / (1.0 + jnp.exp(-g))


def _stack_maps(q):
    lane = lax.broadcasted_iota(jnp.int32, q.shape, 1)
    zero = jnp.zeros_like(q)
    return jnp.concatenate([jnp.where(lane < HEAD // 2, q, zero),
                            jnp.where(lane >= HEAD // 2, q, zero)], axis=0)


def _self_attn_kernel(*refs, mode, tq, hp, lam_init):
    it = iter(refs)
    q_ref, k_ref, vt_ref, g_ref, tile_ref = next(it), next(it), next(it), next(it), next(it)
    if mode == "diff":
        slope_ref, lamv_ref, subg_ref = next(it), next(it), next(it)
    else:
        c_ref = next(it)
    o_ref = next(it)
    kaug_ref, qaug_ref, m_ref, l_ref, acc_ref, sa_ref, sb_ref = (next(it) for _ in range(7))

    hg = pl.program_id(1)
    nq = vt_ref.shape[1]
    tk = tq
    n = qaug_ref.shape[1]
    nhead = n // hp
    npairs = nq // 2
    if mode == "diff":
        bscales = [slope_ref[hg * hp + hs] * LOG2E for hs in range(hp)]

    lane = lax.broadcasted_iota(jnp.int32, (tk, HEAD), 1)
    qlane = lax.broadcasted_iota(jnp.int32, (nhead, HEAD), 1)
    for hs in range(hp):
        if mode == "diff":
            s1, s2, s3 = (piece.astype(F32)
                          for piece in _split3(jnp.full((1, HEAD), bscales[hs], F32)))
            ones = jnp.where(qlane < 2, s1, jnp.where(qlane < 4, s2,
                                                      jnp.where(qlane < 6, s3, 0.0))).astype(BF16)
        else:
            ones = jnp.where(qlane < N_BIAS, 1.0, 0.0).astype(BF16)
        hl = slice(hs * HEAD, (hs + 1) * HEAD)
        for j in range(nq):
            rows = slice(j * tk, (j + 1) * tk)
            if mode == "diff":
                pos = lax.broadcasted_iota(jnp.int32, (tk, HEAD), 0) + j * tk
                piece = jnp.where(lane % 2 == 0, (pos // 16) * 16, pos % 16)
                extra = jnp.where(lane < 6, piece, 0).astype(F32)
            else:
                c8 = c_ref[0, rows, :]
                sel = lax.broadcasted_iota(jnp.int32, c8.shape, 1) == hg * hp + hs
                bias = -LOG2E * jnp.sum(jnp.where(sel, c8, 0.0), axis=1, keepdims=True)
                b1, b2, b3 = (piece.astype(F32) for piece in _split3(bias))
                extra = jnp.where(lane == 0, b1, jnp.where(lane == 1, b2,
                                                           jnp.where(lane == 2, b3, 0.0)))
            kaug_ref[hs, rows, :HEAD] = k_ref[0, rows, hl]
            kaug_ref[hs, rows, HEAD:] = extra.astype(BF16)
            q = q_ref[0, rows, hl]
            qs = _stack_maps(q) if mode == "diff" else q
            qaug_ref[j, hs * nhead:(hs + 1) * nhead, :] = jnp.concatenate([qs, ones], axis=1)
    m_ref[...] = jnp.full(m_ref.shape, NEG, F32)
    l_ref[...] = jnp.ones(l_ref.shape, F32)
    acc_ref[...] = jnp.zeros(acc_ref.shape, F32)

    groups = [(c0 // nhead, slice(c0, c0 + QCOLS)) for c0 in range(0, n, QCOLS)]

    tile = tile_ref[...]
    tiles = [bscales[hs] * tile if mode == "diff" else tile for hs in range(hp)]

    def product(p, s):
        if isinstance(s, int) and s == nq - 1:
            return p, p
        if isinstance(s, int) and s == nq:
            return nq - 1 - p, nq - 1 - p
        first = s < p
        return jnp.where(first, p, nq - 1 - p), jnp.where(first, s, s - p)

    def is_diag(s):
        return isinstance(s, int) and s >= nq - 1

    def nkeys(s, group):
        low_queries = (group[1].start % nhead) % tq + QCOLS <= tq // 2
        return tk // 2 if is_diag(s) and low_queries else tk

    def scores(p, s, group):
        hs, cols = group
        qb, j = product(p, s)
        nk = nkeys(s, group)
        k0 = j * tk if isinstance(j, int) else pl.multiple_of(j * tk, tk)
        sc = lax.dot_general(kaug_ref[hs, pl.ds(k0, nk), :], qaug_ref[qb, cols, :], _NT,
                             preferred_element_type=F32)
        if is_diag(s):
            t0 = (cols.start % nhead) % tq
            sc = sc + tiles[hs][:nk, t0:t0 + QCOLS]
        return sc

    def consume(s_ref, p, s, group):
        hs, cols = group
        qb, j = product(p, s)
        nk = nkeys(s, group)
        sc = s_ref[:nk, cols]
        m_prev = m_ref[qb, :, cols]
        m_new = jnp.maximum(m_prev, jnp.max(sc, axis=0, keepdims=True))
        alpha = jnp.exp2(m_prev - m_new)
        pr = jnp.exp2(sc - m_new)
        l_ref[qb, :, cols] = alpha * l_ref[qb, :, cols] + jnp.sum(pr, axis=0, keepdims=True)
        acc_ref[qb, :, cols] = alpha * acc_ref[qb, :, cols] + jnp.dot(
            vt_ref[hs, j, :, :nk], pr.astype(BF16), preferred_element_type=F32)
        m_ref[qb, :, cols] = m_new

    def pipe_step(nxt, cur, nxt_ref, cur_ref):
        for gi, group in enumerate(groups):
            sc = scores(*nxt, group)
            nxt_ref[:sc.shape[0], group[1]] = sc
            if cur is not None and gi >= 1:
                consume(cur_ref, *cur, groups[gi - 1])
        if cur is not None:
            consume(cur_ref, *cur, groups[-1])

    def finalize_block(blk):
        rows = pl.ds(blk * tq if isinstance(blk, int) else pl.multiple_of(blk * tq, tq), tq)
        for hs in range(hp):
            hl = slice(hs * HEAD, (hs + 1) * HEAD)
            hc = slice(hs * nhead, (hs + 1) * nhead)
            ot = acc_ref[blk, :, hc] * (1.0 / l_ref[blk, :, hc])
            if mode == "diff":
                ot = ot[:, :tq] - _lambda(lamv_ref, lam_init) * ot[:, tq:]
            o = ot.T
            if mode == "diff":
                o = o * _rms_scale(o) * subg_ref[...] * (1.0 - lam_init)
            o_ref[0, rows, hl] = (_gate(g_ref[0, rows, hl]) * o).astype(o_ref.dtype)

    def run_pair(p, a_ref, b_ref):
        def two_slots(i, carry):
            pipe_step((p, 2 * i + 1), (p, 2 * i), b_ref, a_ref)
            pipe_step((p, 2 * i + 2), (p, 2 * i + 1), a_ref, b_ref)
            return carry

        lax.fori_loop(0, (nq - 2) // 2, two_slots, 0)
        prev = jnp.maximum(p - 1, 0)
        finalize_block(prev)
        finalize_block(nq - 1 - prev)
        pipe_step((p, nq - 1), (p, nq - 2), b_ref, a_ref)
        pipe_step((p, nq), (p, nq - 1), a_ref, b_ref)
        pipe_step((jnp.minimum(p + 1, npairs - 1), 0), (p, nq), b_ref, a_ref)

    pipe_step((0, 0), None, sa_ref, None)

    def two_pairs(i, carry):
        run_pair(2 * i, sa_ref, sb_ref)
        run_pair(2 * i + 1, sb_ref, sa_ref)
        return carry

    lax.fori_loop(0, npairs // 2, two_pairs, 0)

    finalize_block(npairs - 1)
    finalize_block(nq - npairs)


def _self_attn_call(mode, q, k, vt, g, tile, extra, *, tq, hp, lam_init=0.0):
    b, t, w = q.shape
    nh = w // HEAD
    nq = t // tq
    assert nq % 4 == 0 and nh % hp == 0
    assert vt.shape == (nh, b * nq, HEAD, tq)
    n = hp * (2 * tq if mode == "diff" else tq)
    seq_spec = pl.BlockSpec((1, t, hp * HEAD), lambda bi, hi: (bi, 0, hi))
    const2 = lambda bi, hi: (0, 0)
    in_specs = [seq_spec, seq_spec,
                pl.BlockSpec((hp, nq, HEAD, tq), lambda bi, hi: (hi, bi, 0, 0)),
                seq_spec,
                pl.BlockSpec((tq, tq), const2, pipeline_mode=pl.Buffered(1))]
    args = [q, k, vt, g, tile]
    if mode == "diff":
        slopes, lamv, subg = extra
        in_specs += [pl.BlockSpec(memory_space=pltpu.SMEM),
                     pl.BlockSpec((4, LANES), const2),
                     pl.BlockSpec((1, HEAD), const2)]
        args += [slopes, lamv, subg]
    else:
        c = extra
        in_specs += [pl.BlockSpec((1, t, c.shape[2]), lambda bi, hi: (bi, 0, 0),
                                  pipeline_mode=pl.Buffered(1))]
        args += [c]
    return pl.pallas_call(
        functools.partial(_self_attn_kernel, mode=mode, tq=tq, hp=hp, lam_init=lam_init),
        grid=(b, nh // hp),
        in_specs=in_specs,
        out_specs=seq_spec,
        out_shape=jax.ShapeDtypeStruct((b, t, w), BF16),
        scratch_shapes=[pltpu.VMEM((hp, t, 2 * HEAD), BF16),
                        pltpu.VMEM((nq, n, 2 * HEAD), BF16),
                        pltpu.VMEM((nq, 1, n), F32), pltpu.VMEM((nq, 1, n), F32),
                        pltpu.VMEM((nq, HEAD, n), F32),
                        pltpu.VMEM((tq, n), F32), pltpu.VMEM((tq, n), F32)],
        compiler_params=_cparams(2),
        name=f"attn_{mode}_self",
    )(*args)


def _cache_attn_kernel(*refs, mode, nh, lp, lam_init):
    it = iter(refs)
    q_ref, kc_ref, vc_ref, kn_ref, vn_ref, g_ref, tile_ref = (next(it) for _ in range(7))
    if mode == "diff":
        slope_ref, lamv_ref, subg_ref = next(it), next(it), next(it)
    else:
        cp_ref, cd_ref = next(it), next(it)
    o_ref = next(it)
    tq = q_ref.shape[1]
    tile = tile_ref[...]
    col = lax.broadcasted_iota(jnp.int32, (1, lp), 1).astype(F32)
    if mode == "diff":
        kn_all = kn_ref[0].reshape(tq, nh * HEAD).astype(BF16)

    for h in range(nh):
        lanes = slice(h * HEAD, (h + 1) * HEAD)
        heads = pl.ds(h, lp, stride=nh)
        q = q_ref[0, :, lanes]
        new_heads = pl.ds(h, tq, stride=nh)
        vn = vn_ref[0, new_heads, :].astype(BF16)
        vc = vc_ref[0, heads, :].astype(BF16)
        if mode == "diff":
            qs = _stack_maps(q)
            kc_t = kc_ref[0, 2 * h:2 * h + 2].reshape(HEAD, lp).astype(BF16)
            kn = kn_all[:, lanes]
            bscale = slope_ref[h] * LOG2E
            bias_c = bscale * (col - float(lp))
            bias_n = bscale * jnp.concatenate([tile, tile], axis=0)
        else:
            qs = q
            kc = kc_ref[0, heads, :].astype(BF16)
            kn = kn_ref[0, new_heads, :].astype(BF16)
            cn = cd_ref[0, h:h + 1, :]
            cref = cn[:, 0:1]
            bias_c = LOG2E * (cref - cp_ref[0, h:h + 1, :])
            bias_n = tile + LOG2E * (cref - cn)
        if mode == "diff":
            s_c = jnp.dot(qs, kc_t, preferred_element_type=F32) + bias_c
        else:
            s_c = lax.dot_general(qs, kc, _NT, preferred_element_type=F32) + bias_c
        s_n = lax.dot_general(qs, kn, _NT, preferred_element_type=F32) + bias_n
        m = jnp.maximum(jnp.max(s_c, axis=1, keepdims=True), jnp.max(s_n, axis=1, keepdims=True))
        p_c = jnp.exp2(s_c - m)
        p_n = jnp.exp2(s_n - m)
        l = jnp.sum(p_c, axis=1, keepdims=True) + jnp.sum(p_n, axis=1, keepdims=True)
        acc = (jnp.dot(p_c.astype(BF16), vc, preferred_element_type=F32)
               + jnp.dot(p_n.astype(BF16), vn, preferred_element_type=F32))
        o = acc / l
        if mode == "diff":
            o = o[:tq] - _lambda(lamv_ref, lam_init) * o[tq:]
            o = o * _rms_scale(o) * subg_ref[...] * (1.0 - lam_init)
        o_ref[0, :, lanes] = (_gate(g_ref[0, :, lanes]) * o).astype(o_ref.dtype)


def _cache_attn_call(mode, q, kc, vc, kn, vn, g, tile, extra, *, nh, lp, lam_init=0.0):
    b, tq, w = q.shape
    row3 = lambda bi: (bi, 0, 0)
    const2 = lambda bi: (0, 0)
    new_spec = pl.BlockSpec((1, tq, w), row3)
    whole = lambda a: pl.BlockSpec((1,) + a.shape[1:], lambda bi: (bi,) + (0,) * (a.ndim - 1))
    in_specs = [new_spec, whole(kc), whole(vc), whole(kn), whole(vn), new_spec,
                pl.BlockSpec((tq, tq), const2)]
    args = [q, kc, vc, kn, vn, g, tile]
    if mode == "diff":
        slopes, lamv, subg = extra
        in_specs += [pl.BlockSpec(memory_space=pltpu.SMEM),
                     pl.BlockSpec((4, LANES), const2),
                     pl.BlockSpec((1, HEAD), const2)]
        args += [slopes, lamv, subg]
    else:
        cp, cd = extra
        in_specs += [pl.BlockSpec((1,) + cp.shape[1:], row3),
                     pl.BlockSpec((1,) + cd.shape[1:], row3)]
        args += [cp, cd]
    return pl.pallas_call(
        functools.partial(_cache_attn_kernel, mode=mode, nh=nh, lp=lp, lam_init=lam_init),
        grid=(b,),
        in_specs=in_specs,
        out_specs=new_spec,
        out_shape=jax.ShapeDtypeStruct((b, tq, w), BF16),
        compiler_params=_cparams(1),
        name=f"attn_{mode}_cache",
    )(*args)


def _out_kernel(x_ref, ma_ref, mb_ref, w_ref, gf_ref, o_ref, wbf_ref):
    @pl.when(pl.program_id(0) == 0)
    def _():
        wbf_ref[...] = w_ref[...].astype(BF16)

    da = ma_ref.shape[1]
    hres = (x_ref[...]
            + jnp.dot(ma_ref[...], wbf_ref[:da, :], preferred_element_type=F32)
            + jnp.dot(mb_ref[...], wbf_ref[da:, :], preferred_element_type=F32))
    o_ref[...] = hres * _rms_scale(hres) * gf_ref[...]


def _out_call(x2, ma, mb, w, gf, tm):
    m, d = x2.shape
    da, db = ma.shape[1], mb.shape[1]
    return pl.pallas_call(
        _out_kernel,
        grid=(m // tm,),
        in_specs=[pl.BlockSpec((tm, d), lambda i: (i, 0)),
                  pl.BlockSpec((tm, da), lambda i: (i, 0)),
                  pl.BlockSpec((tm, db), lambda i: (i, 0)),
                  pl.BlockSpec((da + db, d), lambda i: (0, 0), pipeline_mode=pl.Buffered(1)),
                  pl.BlockSpec((1, d), lambda i: (0, 0))],
        out_specs=pl.BlockSpec((tm, d), lambda i: (i, 0)),
        out_shape=jax.ShapeDtypeStruct((m, d), F32),
        scratch_shapes=[pltpu.VMEM((da + db, d), BF16)],
        compiler_params=_cparams(1),
        name="out_proj",
    )(x2, ma, mb, w, gf)


def _chunk_allowed(q, k):
    return (k // CHUNK) <= (q // CHUNK)


def _diff_tile_kq(t):
    k = jnp.arange(t)[:, None]
    q = jnp.arange(t)[None, :]
    return jnp.where(_chunk_allowed(q, k), jnp.minimum(0, 2 * (q - k)).astype(F32), NEG)


def _causal_tile_kq(t):
    k = jnp.arange(t)[:, None]
    q = jnp.arange(t)[None, :]
    return jnp.where(k <= q, 0.0, NEG).astype(F32)


def _diff_tile_qk(t):
    q = jnp.arange(t)[:, None]
    k = jnp.arange(t)[None, :]
    return jnp.where(_chunk_allowed(q, k), (q - jnp.abs(q - k)).astype(F32), NEG)


def _causal_tile_qk(t):
    return _causal_tile_kq(t).T


def _layer(x, past, params, *, tm, tq, cum_ch):
    (g_norm, w_bf, wf_hl, b_pad, slopes, lamv, subg, lam_init,
     w_out, g_final, d_a, nh_b) = params
    b, s, d = x.shape
    nh_a = d_a // HEAD
    x2 = x.reshape(b * s, d)
    q_scale_a = float((HEAD // 2) ** -0.5 * LOG2E)
    q_scale_b = float(HEAD ** -0.5 * LOG2E)

    prompt = past is None
    assert not prompt or tm == tq
    outs_a = _proj_call(x2, g_norm, w_bf, 0, d_a, q_scale_a, "transposed" if prompt else True, tm,
                        prompt, k_maps_tokens=s)
    outs_b = _proj_call(x2, g_norm, w_bf, 1, d_a, q_scale_b, False, tm, prompt,
                        logf_args=(wf_hl, b_pad, nh_b))
    qa, ka, va, ga = outs_a[:4]
    qb, kb, vb, gb = outs_b[:4]
    logf = outs_b[-1]

    r3 = lambda a: a.reshape(b, s, a.shape[-1])
    if prompt:
        c = _cumsum_call(r3(logf), cum_ch)
        ma = _self_attn_call("diff", r3(qa), r3(outs_a[4]), outs_a[5], r3(ga), _diff_tile_kq(tq),
                             (slopes, lamv, subg), tq=tq, hp=2, lam_init=lam_init)
        mb = _self_attn_call("fox", r3(qb), r3(outs_b[4]), outs_b[5], r3(gb), _causal_tile_kq(tq),
                             c, tq=tq, hp=2)
    else:
        cak, cav, cbk, cbv, cbf = past
        lp = cak.shape[1]
        assert lp % CHUNK == 0 and s == tq <= CHUNK
        tot = lp + s
        padded = -(-tot // cum_ch) * cum_ch
        lf_all = jnp.concatenate([cbf, r3(logf), jnp.zeros((b, padded - tot, nh_b), F32)], axis=1)
        c = jnp.transpose(_cumsum_call(lf_all, cum_ch), (0, 2, 1))
        slab = lambda a: a.reshape(b, lp * a.shape[2], HEAD)
        cak_t = jnp.transpose(cak, (0, 2, 3, 1))
        new = lambda a: a.reshape(b, -1, HEAD)
        ma = _cache_attn_call("diff", r3(qa), cak_t, slab(cav), ka.reshape((b, s) + ka.shape[1:]),
                              new(va), r3(ga), _diff_tile_qk(tq), (slopes, lamv, subg),
                              nh=nh_a, lp=lp, lam_init=lam_init)
        mb = _cache_attn_call("fox", r3(qb), slab(cbk), slab(cbv), new(kb), new(vb), r3(gb),
                              _causal_tile_qk(tq), (c[:, :, :lp], c[:, :, lp:tot]),
                              nh=nh_b, lp=lp)

    y = _out_call(x2, ma.reshape(b * s, -1), mb.reshape(b * s, -1), w_out, g_final,
                  min(tm, 512))
    if prompt:
        ka = jnp.transpose(ka, (0, 3, 1, 2))
    return y.reshape(b, s, d), ka, va, kb, vb, logf


def kernel(x_prompt, x_sample, cache_a_k, cache_a_v, cache_b_k, cache_b_v, cache_b_logf,
           g_norm, w_in, b_f, lam_q1, lam_k1, lam_q2, lam_k2, sub_gain, w_out, g_final):
    depth, d = g_norm.shape
    assert depth == 1
    nh_a2, hqk = cache_a_k.shape[3], cache_a_k.shape[4]
    nh_a = nh_a2 // 2
    nh_b = cache_b_k.shape[3]
    d_a = nh_a2 * hqk
    assert 2 * hqk == HEAD and cache_b_k.shape[4] == HEAD and d == 2 * d_a
    lyr = 0
    lam_init = 0.8 - 0.6 * math.exp(-0.3 * lyr)

    w = jnp.transpose(w_in[lyr])
    w_bf = w.astype(BF16)
    wf = jnp.pad(w[8 * d_a:], ((0, LANES - nh_b), (0, 0)))
    wf_hi = wf.astype(BF16)
    wf_hl = jnp.concatenate([wf_hi, (wf - wf_hi.astype(F32)).astype(BF16)], axis=0)
    b_pad = jnp.pad(b_f[lyr].astype(F32), (0, LANES - nh_b)).reshape(1, LANES)
    slopes = 2.0 ** (-8.0 * jnp.arange(1, nh_a + 1, dtype=F32) / nh_a)
    lamv = jnp.pad(jnp.stack([lam_q1[lyr], lam_k1[lyr], lam_q2[lyr], lam_k2[lyr]]).astype(F32),
                   ((0, 0), (0, LANES - hqk)))
    subg = sub_gain[lyr].astype(F32).reshape(1, HEAD)
    params = (g_norm[lyr].reshape(1, d), w_bf, wf_hl, b_pad, slopes, lamv, subg, lam_init,
              w_out[lyr], g_final.reshape(1, d), d_a, nh_b)

    bp, sp, _ = x_prompt.shape
    bs, ss, _ = x_sample.shape
    yp, pak, pav, pbk, pbv, pbf = _layer(x_prompt, None, params, tm=512, tq=512, cum_ch=256)
    past = (cache_a_k[lyr], cache_a_v[lyr], cache_b_k[lyr], cache_b_v[lyr], cache_b_logf[lyr])
    ys, sak, sav, sbk, sbv, sbf = _layer(x_sample, past, params, tm=bs * ss, tq=ss, cum_ch=256)

    shp = lambda a, bb, s_, n, e: a.reshape(1, bb, s_, n, e)
    return (yp, ys,
            shp(pak, bp, sp, nh_a2, hqk), shp(pav, bp, sp, nh_a, HEAD),
            shp(pbk, bp, sp, nh_b, HEAD), shp(pbv, bp, sp, nh_b, HEAD),
            pbf.reshape(1, bp, sp, nh_b),
            shp(sak, bs, ss, nh_a2, hqk), shp(sav, bs, ss, nh_a, HEAD),
            shp(sbk, bs, ss, nh_b, HEAD), shp(sbv, bs, ss, nh_b, HEAD),
            sbf.reshape(1, bs, ss, nh_b))
```
